```python
import math
import jax, jax.numpy as jnp
from jax import lax
import numpy as np

D_MODEL = 1024
BATCH = 16
SEQ = 2048
DEPTH = 4

N_MIXERS = 4
GRID_W = 64
Q_BLOCK = 128
LN_EPS = 1e-5
RMS_EPS = 1e-6
DEEPNORM_ALPHA = (2 * DEPTH) ** 0.25
DEEPNORM_BETA = (8 * DEPTH) ** -0.25
D_FF = int(math.ceil(8 * D_MODEL / 3 / 256)) * 256
CONV_WIDTH = 3
DA_HEADS = 8
DA_HEAD_DIM = D_MODEL // (2 * DA_HEADS)
NA_HEADS = 16
NA_HEAD_DIM = D_MODEL // NA_HEADS
NA_MAX_ROWS = 8
NA_WIN_COLS = 16
MLA_HEADS = 16
MLA_Q_RANK = 256
MLA_KV_RANK = 128
MLA_NOPE = 64
MLA_ROPE = 32
MLA_V = 64
ROPE_THETA = 10000.0

kernel_name = "hybrid_interleaved_encoder_block"


def _n_uses(m):
    return len(range(m, DEPTH, N_MIXERS))


def _layer_norm(x, g, b):
    xf = x.astype(jnp.float32)
    mu = jnp.mean(xf, axis=-1, keepdims=True)
    var = jnp.mean(jnp.square(xf - mu), axis=-1, keepdims=True)
    y = (xf - mu) * lax.rsqrt(var + LN_EPS) * g.astype(jnp.float32) + b.astype(jnp.float32)
    return y.astype(x.dtype)


def _rms_norm(x, g):
    xf = x.astype(jnp.float32)
    y = xf * lax.rsqrt(jnp.mean(jnp.square(xf), axis=-1, keepdims=True) + RMS_EPS) * g.astype(jnp.float32)
    return y.astype(x.dtype)


def _to_blocks(t):
    b, s = t.shape[:2]
    return jnp.moveaxis(t.reshape((b, s // Q_BLOCK, Q_BLOCK) + t.shape[2:]), 1, 0)


def _from_blocks(t):
    t = jnp.moveaxis(t, 0, 1)
    return t.reshape((t.shape[0], t.shape[1] * t.shape[2]) + t.shape[3:])


def _alibi_slopes(n):
    return np.array([2.0 ** (-8.0 * (h + 1) / n) for h in range(n)], dtype=np.float32)


def _short_conv(x, w_in, conv_w, w_out):
    s = x.shape[1]
    bg, cg, h = jnp.split(x @ w_in, 3, axis=-1)
    u = jnp.pad(cg * h, ((0, 0), (1, 1), (0, 0)))
    y = conv_w[0] * u[:, 0:s] + conv_w[1] * u[:, 1:s + 1] + conv_w[2] * u[:, 2:s + 2]
    return (bg * y) @ w_out


def _diff_attention(x, w_qkv, lam, subln_g, w_out, layer_idx):
    b, s, _ = x.shape
    q, k, v = jnp.split(x @ w_qkv, 3, axis=-1)
    q = q.reshape(b, s, DA_HEADS, 2, DA_HEAD_DIM)
    k = k.reshape(b, s, DA_HEADS, 2, DA_HEAD_DIM)
    v = v.reshape(b, s, DA_HEADS, 2 * DA_HEAD_DIM)
    lam_init = 0.8 - 0.6 * math.exp(-0.3 * layer_idx)
    lamf = lam.astype(jnp.float32)
    lam_full = jnp.exp(jnp.sum(lamf[0] * lamf[1])) - jnp.exp(jnp.sum(lamf[2] * lamf[3])) + lam_init
    slopes = jnp.asarray(_alibi_slopes(DA_HEADS))[:, None, None, None]
    pos = jnp.arange(s)
    scale = DA_HEAD_DIM ** -0.5

    def block(args):
        qb, pb = args
        sc = jnp.einsum('bqhmd,bkhmd->bhmqk', qb, k).astype(jnp.float32) * scale
        dist = jnp.abs(pb[:, None] - pos[None, :]).astype(jnp.float32)
        p = jax.nn.softmax(sc - slopes * dist, axis=-1)
        a = p[:, :, 0] - lam_full * p[:, :, 1]
        return jnp.einsum('bhqk,bkhe->bqhe', a.astype(v.dtype), v)

    o = _from_blocks(lax.map(block, (_to_blocks(q), pos.reshape(-1, Q_BLOCK))))
    o = _rms_norm(o, subln_g) * (1.0 - lam_init)
    return o.reshape(b, s, -1) @ w_out


def _neighborhood_attention(x, w_qkv, rpb, w_out):
    b, s, _ = x.shape
    rows = s // GRID_W
    kr = min(NA_MAX_ROWS, rows)
    kc = NA_WIN_COLS
    q, k, v = jnp.split(x @ w_qkv, 3, axis=-1)
    grid = (b, rows, GRID_W, NA_HEADS, NA_HEAD_DIM)
    q, k, v = q.reshape(grid), k.reshape(grid), v.reshape(grid)
    col = np.arange(GRID_W)
    cs = np.clip(col - kc // 2, 0, GRID_W - kc)
    col_mask = jnp.asarray((col[None, :] >= cs[:, None]) & (col[None, :] < cs[:, None] + kc))
    dc = np.clip(col[None, :] - col[:, None], -(kc - 1), kc - 1) + (kc - 1)
    rpb_c = rpb[:, :, dc]
    scale = NA_HEAD_DIM ** -0.5

    def row_block(args):
        qr, r = args
        rs = jnp.clip(r - kr // 2, 0, rows - kr)
        kb = lax.dynamic_slice_in_dim(k, rs, kr, axis=1)
        vb = lax.dynamic_slice_in_dim(v, rs, kr, axis=1)
        sc = jnp.einsum('bqhd,bjkhd->bhqjk', qr, kb).astype(jnp.float32) * scale
        dr = rs + jnp.arange(kr) - r + (NA_MAX_ROWS - 1)
        bias = jnp.transpose(rpb_c[:, dr], (0, 2, 1, 3)).astype(jnp.float32)
        sc = jnp.where(col_mask[:, None, :], sc + bias, -jnp.inf)
        p = jax.nn.softmax(sc.reshape(sc.shape[:3] + (kr * GRID_W,)), axis=-1).reshape(sc.shape)
        return jnp.einsum('bhqjk,bjkhd->bqhd', p.astype(vb.dtype), vb)

    o = lax.map(row_block, (jnp.moveaxis(q, 1, 0), jnp.arange(rows)))
    o = jnp.moveaxis(o, 0, 1).reshape(b, s, NA_HEADS * NA_HEAD_DIM)
    return o @ w_out


def _rope(t, cos, sin):
    half = t.shape[-1] // 2
    t1, t2 = t[..., :half], t[..., half:]
    c, sn = cos[:, None, :], sin[:, None, :]
    return jnp.concatenate([t1 * c - t2 * sn, t1 * sn + t2 * c], axis=-1)


def _mla(x, w_a, g_q, g_kv, w_uq, w_ukv, w_out):
    b, s, _ = x.shape
    cq, ckv, k_rope = jnp.split(x @ w_a, [MLA_Q_RANK, MLA_Q_RANK + MLA_KV_RANK], axis=-1)
    cq = _rms_norm(cq, g_q)
    ckv = _rms_norm(ckv, g_kv)
    q = (cq @ w_uq).reshape(b, s, MLA_HEADS, MLA_NOPE + MLA_ROPE)
    kv = (ckv @ w_ukv).reshape(b, s, MLA_HEADS, MLA_NOPE + MLA_V)
    q_nope, q_rope = q[..., :MLA_NOPE], q[..., MLA_NOPE:]
    k_nope, v = kv[..., :MLA_NOPE], kv[..., MLA_NOPE:]
    inv_freq = 1.0 / (ROPE_THETA ** (jnp.arange(0, MLA_ROPE, 2, dtype=jnp.float32) / MLA_ROPE))
    ang = jnp.arange(s, dtype=jnp.float32)[:, None] * inv_freq[None, :]
    cos, sin = jnp.cos(ang).astype(x.dtype), jnp.sin(ang).astype(x.dtype)
    q = jnp.concatenate([q_nope, _rope(q_rope, cos, sin)], axis=-1)
    k_r = jnp.broadcast_to(_rope(k_rope[:, :, None, :], cos, sin), (b, s, MLA_HEADS, MLA_ROPE))
    k = jnp.concatenate([k_nope, k_r], axis=-1)
    scale = (MLA_NOPE + MLA_ROPE) ** -0.5

    def block(qb):
        sc = jnp.einsum('bqhd,bkhd->bhqk', qb, k).astype(jnp.float32) * scale
        p = jax.nn.softmax(sc, axis=-1)
        return jnp.einsum('bhqk,bkhd->bqhd', p.astype(v.dtype), v)

    o = _from_blocks(lax.map(block, _to_blocks(q)))
    return o.reshape(b, s, MLA_HEADS * MLA_V) @ w_out


def _swiglu(x, w_gu, w_down):
    g, u = jnp.split(x @ w_gu, 2, axis=-1)
    return (jax.nn.silu(g) * u) @ w_down


def setup_inputs(seed: int = 0) -> dict:
    key = jax.random.key(seed)
    keys = iter(jax.random.split(key, 32))

    def nrm(shape, scale):
        return jax.random.normal(next(keys), shape, jnp.float32) * scale

    def gain(shape):
        return 1.0 + nrm(shape, 0.01)

    d = D_MODEL
    n0, n1, n2, n3 = _n_uses(0), _n_uses(1), _n_uses(2), _n_uses(3)
    beta = DEEPNORM_BETA
    return {
        "x": nrm((BATCH, SEQ, d), 1.0),
        "conv_w_in": nrm((n0, d, 3 * d), d ** -0.5),
        "conv_w": nrm((n0, CONV_WIDTH, d), CONV_WIDTH ** -0.5),
        "conv_w_out": nrm((n0, d, d), d ** -0.5 * beta),
        "diff_w_qkv": nrm((n1, d, 3 * d), d ** -0.5),
        "diff_lambda": nrm((n1, 4, DA_HEAD_DIM), 0.1),
        "diff_subln_g": gain((n1, 2 * DA_HEAD_DIM)),
        "diff_w_out": nrm((n1, d, d), d ** -0.5 * beta),
        "na_w_qkv": nrm((n2, d, 3 * d), d ** -0.5),
        "na_rpb": nrm((n2, NA_HEADS, 2 * NA_MAX_ROWS - 1, 2 * NA_WIN_COLS - 1), 0.05),
        "na_w_out": nrm((n2, d, d), d ** -0.5 * beta),
        "mla_w_a": nrm((n3, d, MLA_Q_RANK + MLA_KV_RANK + MLA_ROPE), d ** -0.5),
        "mla_g_q": gain((n3, MLA_Q_RANK)),
        "mla_g_kv": gain((n3, MLA_KV_RANK)),
        "mla_w_uq": nrm((n3, MLA_Q_RANK, MLA_HEADS * (MLA_NOPE + MLA_ROPE)), MLA_Q_RANK ** -0.5),
        "mla_w_ukv": nrm((n3, MLA_KV_RANK, MLA_HEADS * (MLA_NOPE + MLA_V)), MLA_KV_RANK ** -0.5),
        "mla_w_out": nrm((n3, MLA_HEADS * MLA_V, d), (MLA_HEADS * MLA_V) ** -0.5 * beta),
        "ln1_g": gain((DEPTH, d)),
        "ln1_b": nrm((DEPTH, d), 0.01),
        "ffn_w_gu": nrm((DEPTH, d, 2 * D_FF), d ** -0.5),
        "ffn_w_down": nrm((DEPTH, D_FF, d), D_FF ** -0.5 * beta),
        "ln2_g": gain((DEPTH, d)),
        "ln2_b": nrm((DEPTH, d), 0.01),
    }


def reference(x, conv_w_in, conv_w, conv_w_out, diff_w_qkv, diff_lambda, diff_subln_g, diff_w_out,
              na_w_qkv, na_rpb, na_w_out, mla_w_a, mla_g_q, mla_g_kv, mla_w_uq, mla_w_ukv, mla_w_out,
              ln1_g, ln1_b, ffn_w_gu, ffn_w_down, ln2_g, ln2_b):
    for i in range(DEPTH):
        m, j = i % N_MIXERS, i // N_MIXERS
        if m == 0:
            h = _short_conv(x, conv_w_in[j], conv_w[j], conv_w_out[j])
        elif m == 1:
            h = _diff_attention(x, diff_w_qkv[j], diff_lambda[j], diff_subln_g[j], diff_w_out[j], i)
        elif m == 2:
            h = _neighborhood_attention(x, na_w_qkv[j], na_rpb[j], na_w_out[j])
        else:
            h = _mla(x, mla_w_a[j], mla_g_q[j], mla_g_kv[j], mla_w_uq[j], mla_w_ukv[j], mla_w_out[j])
        x = _layer_norm(DEEPNORM_ALPHA * x + h, ln1_g[i], ln1_b[i])
        x = _layer_norm(DEEPNORM_ALPHA * x + _swiglu(x, ffn_w_gu[i], ffn_w_down[i]), ln2_g[i], ln2_b[i])
    return x
```

```python
import functools
import math

import numpy as np
import jax
import jax.numpy as jnp
from jax import lax
from jax.experimental import pallas as pl
from jax.experimental.pallas import tpu as pltpu

DEPTH = 4
N_MIXERS = 4
GRID_W = 64
LN_EPS = 1e-5
RMS_EPS = 1e-6
DEEPNORM_ALPHA = (2 * DEPTH) ** 0.25
DA_HEADS = 8
NA_HEADS = 16
NA_MAX_ROWS = 8
NA_WIN_COLS = 16
MLA_HEADS = 16
MLA_Q_RANK = 256
MLA_KV_RANK = 128
MLA_NOPE = 64
MLA_ROPE = 32
MLA_V = 64
ROPE_THETA = 10000.0

LANES = 128
VMEM_LIMIT = 56 * 1024 * 1024
BF16 = jnp.bfloat16
F32 = jnp.float32


def _params(*sem):
    return pltpu.CompilerParams(dimension_semantics=sem, vmem_limit_bytes=VMEM_LIMIT)


def _layer_norm_rows(y, g, b):
    mu = jnp.mean(y, axis=-1, keepdims=True)
    yc = y - mu
    var = jnp.mean(yc * yc, axis=-1, keepdims=True)
    return yc * lax.rsqrt(var + LN_EPS) * g + b


def _nt_dot(a, b):
    return lax.dot_general(a, b, (((1,), (1,)), ((), ())), preferred_element_type=F32)


def _proj_kernel(x_ref, w_ref, o_ref, xb_ref):
    @pl.when(pl.program_id(1) == 0)
    def _():
        xb_ref[...] = x_ref[...].astype(BF16)

    o_ref[...] = jnp.dot(xb_ref[...], w_ref[...], preferred_element_type=F32).astype(o_ref.dtype)


def _proj(x, w, out_dtype, tm=1024, tn=1024):
    t, k = x.shape
    n = w.shape[1]
    tm, tn = min(tm, t), min(tn, n)
    return pl.pallas_call(
        _proj_kernel,
        grid=(t // tm, n // tn),
        in_specs=[pl.BlockSpec((tm, k), lambda i, j: (i, 0)),
                  pl.BlockSpec((k, tn), lambda i, j: (0, j))],
        out_specs=pl.BlockSpec((tm, tn), lambda i, j: (i, j)),
        out_shape=jax.ShapeDtypeStruct((t, n), out_dtype),
        scratch_shapes=[pltpu.VMEM((tm, k), BF16)],
        compiler_params=_params("parallel", "arbitrary"),
        name="proj",
    )(x, w)


def _proj_res_ln_kernel(a_ref, w_ref, x_ref, g_ref, b_ref, o_ref):
    h = jnp.dot(a_ref[...], w_ref[...], preferred_element_type=F32)
    y = DEEPNORM_ALPHA * x_ref[...] + h
    o_ref[...] = _layer_norm_rows(y, g_ref[...], b_ref[...])


def _proj_res_ln(a, w, x, g, b, tm=512):
    t, k = a.shape
    d = w.shape[1]
    tm = min(tm, t)
    return pl.pallas_call(
        _proj_res_ln_kernel,
        grid=(t // tm,),
        in_specs=[pl.BlockSpec((tm, k), lambda i: (i, 0)),
                  pl.BlockSpec((k, d), lambda i: (0, 0)),
                  pl.BlockSpec((tm, d), lambda i: (i, 0)),
                  pl.BlockSpec((1, d), lambda i: (0, 0)),
                  pl.BlockSpec((1, d), lambda i: (0, 0))],
        out_specs=pl.BlockSpec((tm, d), lambda i: (i, 0)),
        out_shape=jax.ShapeDtypeStruct((t, d), F32),
        compiler_params=_params("parallel"),
        name="proj_res_ln",
    )(a, w, x, g.reshape(1, d), b.reshape(1, d))


def _ffn_kernel(x_ref, wgu_ref, wd_ref, g_ref, b_ref, o_ref, *, d_ff, tf):
    x = x_ref[...]
    xb = x.astype(BF16)
    acc = jnp.zeros(x.shape, F32)
    for c in range(d_ff // tf):
        gate = jnp.dot(xb, wgu_ref[:, c * tf:(c + 1) * tf], preferred_element_type=F32)
        up = jnp.dot(xb, wgu_ref[:, d_ff + c * tf:d_ff + (c + 1) * tf], preferred_element_type=F32)
        h = (gate * jax.nn.sigmoid(gate) * up).astype(BF16)
        acc = acc + jnp.dot(h, wd_ref[c * tf:(c + 1) * tf, :], preferred_element_type=F32)
    y = DEEPNORM_ALPHA * x + acc
    o_ref[...] = _layer_norm_rows(y, g_ref[...], b_ref[...])


def _ffn(x, w_gu, w_down, g, b, tm=512, tf=256):
    t, d = x.shape
    d_ff = w_down.shape[0]
    tm = min(tm, t)
    return pl.pallas_call(
        functools.partial(_ffn_kernel, d_ff=d_ff, tf=tf),
        grid=(t // tm,),
        in_specs=[pl.BlockSpec((tm, d), lambda i: (i, 0)),
                  pl.BlockSpec((d, 2 * d_ff), lambda i: (0, 0)),
                  pl.BlockSpec((d_ff, d), lambda i: (0, 0)),
                  pl.BlockSpec((1, d), lambda i: (0, 0)),
                  pl.BlockSpec((1, d), lambda i: (0, 0))],
        out_specs=pl.BlockSpec((tm, d), lambda i: (i, 0)),
        out_shape=jax.ShapeDtypeStruct((t, d), F32),
        compiler_params=_params("parallel"),
        name="ffn",
    )(x, w_gu, w_down, g.reshape(1, d), b.reshape(1, d))


def _conv_kernel(x_ref, wb_ref, wc_ref, wh_ref, cw_ref, o_ref, xb_ref):
    @pl.when(pl.program_id(1) == 0)
    def _():
        xb_ref[...] = x_ref[0].astype(BF16)

    xb = xb_ref[...]
    s = xb.shape[0]
    bg = jnp.dot(xb, wb_ref[...], preferred_element_type=F32)
    cg = jnp.dot(xb, wc_ref[...], preferred_element_type=F32)
    hh = jnp.dot(xb, wh_ref[...], preferred_element_type=F32)
    u = cg * hh
    row = lax.broadcasted_iota(jnp.int32, (s, 1), 0)
    prev = jnp.where(row == 0, 0.0, pltpu.roll(u, 1, axis=0))
    nxt = jnp.where(row == s - 1, 0.0, pltpu.roll(u, s - 1, axis=0))
    cw = cw_ref[...]
    y = cw[0:1, :] * prev + cw[1:2, :] * u + cw[2:3, :] * nxt
    o_ref[0] = (bg * y).astype(o_ref.dtype)


def _conv_mixer(x3, w_in, conv_w, tc=256):
    b, s, d = x3.shape
    nj = d // tc
    return pl.pallas_call(
        _conv_kernel,
        grid=(b, nj),
        in_specs=[pl.BlockSpec((1, s, d), lambda i, j: (i, 0, 0)),
                  pl.BlockSpec((d, tc), lambda i, j: (0, j)),
                  pl.BlockSpec((d, tc), lambda i, j: (0, nj + j)),
                  pl.BlockSpec((d, tc), lambda i, j: (0, 2 * nj + j)),
                  pl.BlockSpec((3, tc), lambda i, j: (0, j))],
        out_specs=pl.BlockSpec((1, s, tc), lambda i, j: (i, 0, j)),
        out_shape=jax.ShapeDtypeStruct((b, s, d), BF16),
        scratch_shapes=[pltpu.VMEM((s, d), BF16)],
        compiler_params=_params("parallel", "arbitrary"),
        name="conv_mixer",
    )(x3, w_in, w_in, w_in, conv_w)


def _diff_attn_kernel(q_ref, k_ref, v_ref, lam_ref, slope_ref, g_ref, o_ref, *, lam_init):
    tq = q_ref.shape[1]
    s = k_ref.shape[1]
    hd = LANES // 2
    q = q_ref[0]
    lane = lax.broadcasted_iota(jnp.int32, (1, LANES), 1)
    scale = hd ** -0.5
    m1 = jnp.where(lane < hd, scale, 0.0).astype(BF16)
    m2 = jnp.where(lane >= hd, scale, 0.0).astype(BF16)
    qq = jnp.concatenate([q * m1, q * m2], axis=0)
    sc = _nt_dot(qq, k_ref[0])
    slope = slope_ref[0][:, 0:1]
    rowf = (pl.program_id(2) * tq + lax.broadcasted_iota(jnp.int32, (tq, 1), 0)).astype(F32)
    colf = lax.broadcasted_iota(jnp.int32, (1, s), 1).astype(F32)
    bias = jnp.abs(slope * rowf - slope * colf)
    s1 = sc[:tq] - bias
    s2 = sc[tq:] - bias
    e1 = jnp.exp(s1 - jnp.max(s1, axis=-1, keepdims=True))
    e2 = jnp.exp(s2 - jnp.max(s2, axis=-1, keepdims=True))
    lam = lam_ref[...]
    lam_full = (jnp.exp(jnp.sum(lam[0:1] * lam[1:2], axis=-1, keepdims=True))
                - jnp.exp(jnp.sum(lam[2:3] * lam[3:4], axis=-1, keepdims=True)) + lam_init)
    r1 = 1.0 / jnp.sum(e1, axis=-1, keepdims=True)
    r2 = lam_full / jnp.sum(e2, axis=-1, keepdims=True)
    a = (e1 * r1 - e2 * r2).astype(BF16)
    o = jnp.dot(a, v_ref[0], preferred_element_type=F32)
    ms = jnp.mean(o * o, axis=-1, keepdims=True)
    o = o * lax.rsqrt(ms + RMS_EPS) * g_ref[...] * (1.0 - lam_init)
    o_ref[0] = o.astype(o_ref.dtype)


def _diff_attention(qkv, lam, subln_g, layer_idx, tq=256):
    b, s, d3 = qkv.shape
    d = d3 // 3
    nh = d // LANES
    tq = min(tq, s)
    lam_init = 0.8 - 0.6 * math.exp(-0.3 * layer_idx)
    slopes = np.array([2.0 ** (-8.0 * (h + 1) / nh) for h in range(nh)], dtype=np.float32)
    slopes = jnp.asarray(np.broadcast_to(slopes[:, None, None], (nh, 1, LANES)).copy())
    return pl.pallas_call(
        functools.partial(_diff_attn_kernel, lam_init=lam_init),
        grid=(b, nh, s // tq),
        in_specs=[pl.BlockSpec((1, tq, LANES), lambda i, h, j: (i, j, h)),
                  pl.BlockSpec((1, s, LANES), lambda i, h, j: (i, 0, nh + h)),
                  pl.BlockSpec((1, s, LANES), lambda i, h, j: (i, 0, 2 * nh + h)),
                  pl.BlockSpec(lam.shape, lambda i, h, j: (0, 0)),
                  pl.BlockSpec((1, 1, LANES), lambda i, h, j: (h, 0, 0)),
                  pl.BlockSpec((1, LANES), lambda i, h, j: (0, 0))],
        out_specs=pl.BlockSpec((1, tq, LANES), lambda i, h, j: (i, j, h)),
        out_shape=jax.ShapeDtypeStruct((b, s, d), BF16),
        compiler_params=_params("parallel", "parallel", "arbitrary"),
        name="diff_attn",
    )(qkv, qkv, qkv, lam, slopes, subln_g.reshape(1, LANES))


def _na_kernel(q_ref, k_ref, v_ref, tbl_ref, o_ref, *, rows, kr):
    w = GRID_W
    hd = LANES // 2
    lane = lax.broadcasted_iota(jnp.int32, (1, LANES), 1)
    scale = hd ** -0.5
    m0 = jnp.where(lane < hd, scale, 0.0).astype(BF16)
    m1 = jnp.where(lane >= hd, scale, 0.0).astype(BF16)

    def row_body(r, carry):
        rs = jnp.clip(r - kr // 2, 0, rows - kr)
        win = rs - r + (NA_MAX_ROWS - 1)
        q = q_ref[0, pl.ds(pl.multiple_of(r * w, w), w), :]
        kb = k_ref[0, pl.ds(pl.multiple_of(rs * w, w), kr * w), :]
        vb = v_ref[0, pl.ds(pl.multiple_of(rs * w, w), kr * w), :]
        qq = jnp.concatenate([q * m0, q * m1], axis=0)
        sc = _nt_dot(qq, kb)
        bias = jnp.concatenate(
            [jnp.concatenate([tbl_ref[0, e, win + 2 * jj] for jj in range(kr // 2)], axis=1)
             for e in range(2)], axis=0)
        sc = sc + bias
        e = jnp.exp(sc - jnp.max(sc, axis=-1, keepdims=True))
        rl = 1.0 / jnp.sum(e, axis=-1, keepdims=True)
        o = jnp.dot(e.astype(BF16), vb, preferred_element_type=F32) * rl
        out = jnp.where(lane < hd, o[:w], o[w:])
        o_ref[0, pl.ds(pl.multiple_of(r * w, w), w), :] = out.astype(o_ref.dtype)
        return carry

    lax.fori_loop(0, rows, row_body, 0)


def _na_bias_table(rpb):
    kc = NA_WIN_COLS
    col = np.arange(GRID_W)
    cs = np.clip(col - kc // 2, 0, GRID_W - kc)
    col_mask = (col[None, :] >= cs[:, None]) & (col[None, :] < cs[:, None] + kc)
    dc = np.clip(col[None, :] - col[:, None], -(kc - 1), kc - 1) + (kc - 1)
    bias = jnp.where(jnp.asarray(col_mask), rpb[:, :, dc], -jnp.inf)
    tbl = jnp.concatenate([bias[:, :-1], bias[:, 1:]], axis=-1)
    h = tbl.shape[0]
    return tbl.reshape((h // 2, 2) + tbl.shape[1:])


def _neighborhood_attention(qkv, rpb):
    b, s, d3 = qkv.shape
    d = d3 // 3
    npair = d // LANES
    rows = s // GRID_W
    kr = min(NA_MAX_ROWS, rows)
    tbl = _na_bias_table(rpb.astype(F32))
    return pl.pallas_call(
        functools.partial(_na_kernel, rows=rows, kr=kr),
        grid=(b, npair),
        in_specs=[pl.BlockSpec((1, s, LANES), lambda i, p: (i, 0, p)),
                  pl.BlockSpec((1, s, LANES), lambda i, p: (i, 0, npair + p)),
                  pl.BlockSpec((1, s, LANES), lambda i, p: (i, 0, 2 * npair + p)),
                  pl.BlockSpec((1,) + tbl.shape[1:], lambda i, p: (p, 0, 0, 0, 0))],
        out_specs=pl.BlockSpec((1, s, LANES), lambda i, p: (i, 0, p)),
        out_shape=jax.ShapeDtypeStruct((b, s, d), BF16),
        compiler_params=_params("parallel", "parallel"),
        name="na_attn",
    )(qkv, qkv, qkv, tbl)


def _mla_proj_kernel(x_ref, wa_ref, gq_ref, gkv_ref, wuq_ref, wukv_ref, cos_ref, sin_ref,
                     qn_ref, qr_ref, kn_ref, kr_ref, v_ref, *, scale):
    qrank, kvrank = MLA_Q_RANK, MLA_KV_RANK
    a = jnp.dot(x_ref[...].astype(BF16), wa_ref[...], preferred_element_type=F32)
    cq = a[:, :qrank]
    cq = cq * lax.rsqrt(jnp.mean(cq * cq, axis=-1, keepdims=True) + RMS_EPS) * gq_ref[...]
    ckv = a[:, qrank:qrank + kvrank]
    ckv = ckv * lax.rsqrt(jnp.mean(ckv * ckv, axis=-1, keepdims=True) + RMS_EPS) * gkv_ref[...]
    cos = cos_ref[...]
    sin = sin_ref[...]
    o = qrank + kvrank
    kr_ref[...] = (a[:, o:o + LANES] * cos[:, :LANES] + a[:, o + LANES:o + 2 * LANES] * sin[:, :LANES]
                   ).astype(kr_ref.dtype)
    q = jnp.dot(cq.astype(BF16), wuq_ref[...], preferred_element_type=F32)
    nn = qn_ref.shape[1]
    nr = qr_ref.shape[1]
    qn_ref[...] = (q[:, :nn] * scale).astype(qn_ref.dtype)
    qr_ref[...] = ((q[:, nn:nn + nr] * cos + q[:, nn + nr:nn + 2 * nr] * sin) * scale).astype(qr_ref.dtype)
    kv = jnp.dot(ckv.astype(BF16), wukv_ref[...], preferred_element_type=F32)
    kn_ref[...] = kv[:, :nn].astype(kn_ref.dtype)
    v_ref[...] = kv[:, nn:].astype(v_ref.dtype)


def _swap_halves(t):
    half = t.shape[-1] // 2
    return jnp.concatenate([t[..., half:], t[..., :half]], axis=-1)


def _mla_projection(x, w_a, g_q, g_kv, w_uq, w_ukv, seq, tm=512):
    t, d = x.shape
    nh, nope, rope, vd = MLA_HEADS, MLA_NOPE, MLA_ROPE, MLA_V
    qrank, kvrank = MLA_Q_RANK, MLA_KV_RANK
    tm = min(tm, seq)
    rep = LANES // rope
    w_kr = w_a[:, qrank + kvrank:]
    wa_ext = jnp.concatenate([w_a[:, :qrank + kvrank], jnp.tile(w_kr, (1, rep)),
                              jnp.tile(_swap_halves(w_kr), (1, rep))], axis=1).astype(BF16)
    wq = w_uq.reshape(qrank, nh, nope + rope)
    wq_rope = wq[:, :, nope:]
    wuq_ext = jnp.concatenate([wq[:, :, :nope].reshape(qrank, nh * nope),
                               wq_rope.reshape(qrank, nh * rope),
                               _swap_halves(wq_rope).reshape(qrank, nh * rope)], axis=1).astype(BF16)
    wkv = w_ukv.reshape(kvrank, nh, nope + vd)
    wukv_p = jnp.concatenate([wkv[:, :, :nope].reshape(kvrank, nh * nope),
                              wkv[:, :, nope:].reshape(kvrank, nh * vd)], axis=1).astype(BF16)
    inv_freq = 1.0 / (ROPE_THETA ** (jnp.arange(0, rope, 2, dtype=F32) / rope))
    ang = jnp.arange(seq, dtype=F32)[:, None] * inv_freq[None, :]
    cos, sin = jnp.cos(ang), jnp.sin(ang)
    cos_t = jnp.tile(jnp.concatenate([cos, cos], axis=-1), (1, nh))
    sin_t = jnp.tile(jnp.concatenate([-sin, sin], axis=-1), (1, nh))
    nblk = seq // tm
    scale = (nope + rope) ** -0.5
    na = wa_ext.shape[1]
    nq = wuq_ext.shape[1]
    nkv = wukv_p.shape[1]
    const = lambda i: (0, 0)
    tok = lambda i: (i, 0)
    return pl.pallas_call(
        functools.partial(_mla_proj_kernel, scale=scale),
        grid=(t // tm,),
        in_specs=[pl.BlockSpec((tm, d), tok),
                  pl.BlockSpec((d, na), const),
                  pl.BlockSpec((1, qrank), const),
                  pl.BlockSpec((1, kvrank), const),
                  pl.BlockSpec((qrank, nq), const),
                  pl.BlockSpec((kvrank, nkv), const),
                  pl.BlockSpec((tm, nh * rope), lambda i: (i % nblk, 0)),
                  pl.BlockSpec((tm, nh * rope), lambda i: (i % nblk, 0))],
        out_specs=[pl.BlockSpec((tm, nh * nope), tok),
                   pl.BlockSpec((tm, nh * rope), tok),
                   pl.BlockSpec((tm, nh * nope), tok),
                   pl.BlockSpec((tm, LANES), tok),
                   pl.BlockSpec((tm, nh * vd), tok)],
        out_shape=[jax.ShapeDtypeStruct((t, nh * nope), BF16),
                   jax.ShapeDtypeStruct((t, nh * rope), BF16),
                   jax.ShapeDtypeStruct((t, nh * nope), BF16),
                   jax.ShapeDtypeStruct((t, LANES), BF16),
                   jax.ShapeDtypeStruct((t, nh * vd), BF16)],
        compiler_params=_params("parallel"),
        name="mla_proj",
    )(x, wa_ext, g_q.reshape(1, qrank), g_kv.reshape(1, kvrank), wuq_ext, wukv_p, cos_t, sin_t)


def _mla_attn_kernel(qn_ref, qr_ref, kn_ref, kr_ref, v_ref, o_ref, kcat_ref):
    tq = qn_ref.shape[1]
    p = pl.program_id(1)

    @pl.when(pl.program_id(2) == 0)
    def _():
        kcat_ref[:, :LANES] = kn_ref[0]
        kcat_ref[:, LANES:] = kr_ref[0]

    lane = lax.broadcasted_iota(jnp.int32, (1, LANES), 1)
    qn = qn_ref[0]
    qr = qr_ref[0]
    lhs = []
    for e in range(2):
        mn = jnp.where((lane >= e * MLA_NOPE) & (lane < (e + 1) * MLA_NOPE), 1.0, 0.0).astype(BF16)
        lo = (p % 2) * (2 * MLA_ROPE) + e * MLA_ROPE
        mr = jnp.where((lane >= lo) & (lane < lo + MLA_ROPE), 1.0, 0.0).astype(BF16)
        lhs.append(jnp.concatenate([qn * mn, qr * mr], axis=1))
    lhs = jnp.concatenate(lhs, axis=0)
    sc = _nt_dot(lhs, kcat_ref[...])
    e = jnp.exp(sc - jnp.max(sc, axis=-1, keepdims=True))
    rl = 1.0 / jnp.sum(e, axis=-1, keepdims=True)
    o = jnp.dot(e.astype(BF16), v_ref[0], preferred_element_type=F32) * rl
    o_ref[0] = jnp.where(lane < MLA_V, o[:tq], o[tq:]).astype(o_ref.dtype)


def _mla_attention(qn, qr, kn, kr, v, tq=256):
    b, s, _ = qn.shape
    npair = MLA_HEADS // 2
    tq = min(tq, s)
    return pl.pallas_call(
        _mla_attn_kernel,
        grid=(b, npair, s // tq),
        in_specs=[pl.BlockSpec((1, tq, LANES), lambda i, p, j: (i, j, p)),
                  pl.BlockSpec((1, tq, LANES), lambda i, p, j: (i, j, p // 2)),
                  pl.BlockSpec((1, s, LANES), lambda i, p, j: (i, 0, p)),
                  pl.BlockSpec((1, s, LANES), lambda i, p, j: (i, 0, 0)),
                  pl.BlockSpec((1, s, LANES), lambda i, p, j: (i, 0, p))],
        out_specs=pl.BlockSpec((1, tq, LANES), lambda i, p, j: (i, j, p)),
        out_shape=jax.ShapeDtypeStruct((b, s, npair * LANES), BF16),
        scratch_shapes=[pltpu.VMEM((s, 2 * LANES), BF16)],
        compiler_params=_params("parallel", "parallel", "arbitrary"),
        name="mla_attn",
    )(qn, qr, kn, kr, v)


def kernel(x, conv_w_in, conv_w, conv_w_out, diff_w_qkv, diff_lambda, diff_subln_g, diff_w_out,
           na_w_qkv, na_rpb, na_w_out, mla_w_a, mla_g_q, mla_g_kv, mla_w_uq, mla_w_ukv, mla_w_out,
           ln1_g, ln1_b, ffn_w_gu, ffn_w_down, ln2_g, ln2_b):
    b, s, d = x.shape
    t = b * s
    xf = x.reshape(t, d)
    for i in range(DEPTH):
        m, j = i % N_MIXERS, i // N_MIXERS
        if m == 0:
            a = _conv_mixer(xf.reshape(b, s, d), conv_w_in[j].astype(BF16), conv_w[j])
            w_out = conv_w_out[j]
        elif m == 1:
            qkv = _proj(xf, diff_w_qkv[j].astype(BF16), BF16)
            a = _diff_attention(qkv.reshape(b, s, 3 * d), diff_lambda[j], diff_subln_g[j], i)
            w_out = diff_w_out[j]
        elif m == 2:
            qkv = _proj(xf, na_w_qkv[j].astype(BF16), BF16)
            a = _neighborhood_attention(qkv.reshape(b, s, 3 * d), na_rpb[j])
            w_out = na_w_out[j]
        else:
            qn, qr, kn, kr, v = _mla_projection(xf, mla_w_a[j], mla_g_q[j], mla_g_kv[j],
                                                mla_w_uq[j], mla_w_ukv[j], s)
            r3 = lambda z: z.reshape(b, s, z.shape[-1])
            a = _mla_attention(r3(qn), r3(qr), r3(kn), r3(kr), r3(v))
            w_out = mla_w_out[j]
        xf = _proj_res_ln(a.reshape(t, -1), w_out.astype(BF16), xf, ln1_g[i], ln1_b[i])
        xf = _ffn(xf, ffn_w_gu[i].astype(BF16), ffn_w_down[i].astype(BF16), ln2_g[i], ln2_b[i])
    return xf.reshape(b, s, d)
```

```python
import functools
import math

import numpy as np
import jax
import jax.numpy as jnp
from jax import lax
from jax.experimental import pallas as pl
from jax.experimental.pallas import tpu as pltpu

DEPTH = 4
N_MIXERS = 4
GRID_W = 64
LN_EPS = 1e-5
RMS_EPS = 1e-6
LOG2E = math.log2(math.e)
DEEPNORM_ALPHA = (2 * DEPTH) ** 0.25
DA_HEADS = 8
NA_HEADS = 16
NA_MAX_ROWS = 8
NA_WIN_COLS = 16
MLA_HEADS = 16
MLA_Q_RANK = 256
MLA_KV_RANK = 128
MLA_NOPE = 64
MLA_ROPE = 32
MLA_V = 64
ROPE_THETA = 10000.0

LANES = 128
VMEM_LIMIT = 56 * 1024 * 1024
BF16 = jnp.bfloat16
F32 = jnp.float32


def _params(*sem):
    return pltpu.CompilerParams(dimension_semantics=sem, vmem_limit_bytes=VMEM_LIMIT)


def _layer_norm_rows(y, g, b):
    mu = jnp.mean(y, axis=-1, keepdims=True)
    yc = y - mu
    var = jnp.mean(yc * yc, axis=-1, keepdims=True)
    return yc * lax.rsqrt(var + LN_EPS) * g + b


def _nt_dot(a, b):
    return lax.dot_general(a, b, (((1,), (1,)), ((), ())), preferred_element_type=F32)


def _proj_kernel(x_ref, w_ref, o_ref, xb_ref):
    @pl.when(pl.program_id(1) == 0)
    def _():
        xb_ref[...] = x_ref[...].astype(BF16)

    o_ref[...] = jnp.dot(xb_ref[...], w_ref[...], preferred_element_type=F32).astype(o_ref.dtype)


def _proj(x, w, out_dtype, tm=1024, tn=1024):
    t, k = x.shape
    n = w.shape[1]
    tm, tn = min(tm, t), min(tn, n)
    return pl.pallas_call(
        _proj_kernel,
        grid=(t // tm, n // tn),
        in_specs=[pl.BlockSpec((tm, k), lambda i, j: (i, 0)),
                  pl.BlockSpec((k, tn), lambda i, j: (0, j))],
        out_specs=pl.BlockSpec((tm, tn), lambda i, j: (i, j)),
        out_shape=jax.ShapeDtypeStruct((t, n), out_dtype),
        scratch_shapes=[pltpu.VMEM((tm, k), BF16)],
        compiler_params=_params("parallel", "arbitrary"),
        name="proj",
    )(x, w)


def _proj_res_ln_kernel(a_ref, w_ref, x_ref, g_ref, b_ref, o_ref):
    h = jnp.dot(a_ref[...], w_ref[...], preferred_element_type=F32)
    y = DEEPNORM_ALPHA * x_ref[...] + h
    o_ref[...] = _layer_norm_rows(y, g_ref[...], b_ref[...])


def _proj_res_ln(a, w, x, g, b, tm=512):
    t, k = a.shape
    d = w.shape[1]
    tm = min(tm, t)
    return pl.pallas_call(
        _proj_res_ln_kernel,
        grid=(t // tm,),
        in_specs=[pl.BlockSpec((tm, k), lambda i: (i, 0)),
                  pl.BlockSpec((k, d), lambda i: (0, 0)),
                  pl.BlockSpec((tm, d), lambda i: (i, 0)),
                  pl.BlockSpec((1, d), lambda i: (0, 0)),
                  pl.BlockSpec((1, d), lambda i: (0, 0))],
        out_specs=pl.BlockSpec((tm, d), lambda i: (i, 0)),
        out_shape=jax.ShapeDtypeStruct((t, d), F32),
        compiler_params=_params("parallel"),
        name="proj_res_ln",
    )(a, w, x, g.reshape(1, d), b.reshape(1, d))


def _ffn_kernel(x_ref, wgu_ref, wd_ref, g_ref, b_ref, o_ref, *, d_ff, tf):
    x = x_ref[...]
    xb = x.astype(BF16)
    acc = jnp.zeros(x.shape, F32)
    for c in range(d_ff // tf):
        gate = jnp.dot(xb, wgu_ref[:, c * tf:(c + 1) * tf], preferred_element_type=F32)
        up = jnp.dot(xb, wgu_ref[:, d_ff + c * tf:d_ff + (c + 1) * tf], preferred_element_type=F32)
        h = (gate * jax.nn.sigmoid(gate) * up).astype(BF16)
        acc = acc + jnp.dot(h, wd_ref[c * tf:(c + 1) * tf, :], preferred_element_type=F32)
    y = DEEPNORM_ALPHA * x + acc
    o_ref[...] = _layer_norm_rows(y, g_ref[...], b_ref[...])


def _ffn(x, w_gu, w_down, g, b, tm=512, tf=256):
    t, d = x.shape
    d_ff = w_down.shape[0]
    tm = min(tm, t)
    return pl.pallas_call(
        functools.partial(_ffn_kernel, d_ff=d_ff, tf=tf),
        grid=(t // tm,),
        in_specs=[pl.BlockSpec((tm, d), lambda i: (i, 0)),
                  pl.BlockSpec((d, 2 * d_ff), lambda i: (0, 0)),
                  pl.BlockSpec((d_ff, d), lambda i: (0, 0)),
                  pl.BlockSpec((1, d), lambda i: (0, 0)),
                  pl.BlockSpec((1, d), lambda i: (0, 0))],
        out_specs=pl.BlockSpec((tm, d), lambda i: (i, 0)),
        out_shape=jax.ShapeDtypeStruct((t, d), F32),
        compiler_params=_params("parallel"),
        name="ffn",
    )(x, w_gu, w_down, g.reshape(1, d), b.reshape(1, d))


def _conv_kernel(x_ref, wb_ref, wc_ref, wh_ref, cw_ref, o_ref, xb_ref):
    @pl.when(pl.program_id(1) == 0)
    def _():
        xb_ref[...] = x_ref[0].astype(BF16)

    xb = xb_ref[...]
    s = xb.shape[0]
    bg = jnp.dot(xb, wb_ref[...], preferred_element_type=F32)
    cg = jnp.dot(xb, wc_ref[...], preferred_element_type=F32)
    hh = jnp.dot(xb, wh_ref[...], preferred_element_type=F32)
    u = cg * hh
    row = lax.broadcasted_iota(jnp.int32, (s, 1), 0)
    prev = jnp.where(row == 0, 0.0, pltpu.roll(u, 1, axis=0))
    nxt = jnp.where(row == s - 1, 0.0, pltpu.roll(u, s - 1, axis=0))
    cw = cw_ref[...]
    y = cw[0:1, :] * prev + cw[1:2, :] * u + cw[2:3, :] * nxt
    o_ref[0] = (bg * y).astype(o_ref.dtype)


def _conv_mixer(x3, w_in, conv_w, tc=256):
    b, s, d = x3.shape
    nj = d // tc
    return pl.pallas_call(
        _conv_kernel,
        grid=(b, nj),
        in_specs=[pl.BlockSpec((1, s, d), lambda i, j: (i, 0, 0)),
                  pl.BlockSpec((d, tc), lambda i, j: (0, j)),
                  pl.BlockSpec((d, tc), lambda i, j: (0, nj + j)),
                  pl.BlockSpec((d, tc), lambda i, j: (0, 2 * nj + j)),
                  pl.BlockSpec((3, tc), lambda i, j: (0, j))],
        out_specs=pl.BlockSpec((1, s, tc), lambda i, j: (i, 0, j)),
        out_shape=jax.ShapeDtypeStruct((b, s, d), BF16),
        scratch_shapes=[pltpu.VMEM((s, d), BF16)],
        compiler_params=_params("parallel", "arbitrary"),
        name="conv_mixer",
    )(x3, w_in, w_in, w_in, conv_w)


def _diff_attn_kernel(q_ref, k_ref, v_ref, lam_ref, slope_ref, g_ref, o_ref, *, lam_init, n_chain):
    tq = q_ref.shape[1]
    s = k_ref.shape[1]
    hd = LANES // 2
    rows = tq // n_chain
    lane = lax.broadcasted_iota(jnp.int32, (1, LANES), 1)
    m1 = jnp.where(lane < hd, 1.0, 0.0).astype(BF16)
    m2 = jnp.where(lane >= hd, 1.0, 0.0).astype(BF16)
    slope = slope_ref[0][:, 0:1] * LOG2E
    lam = lam_ref[...]
    lam_full = (jnp.exp(jnp.sum(lam[0:1] * lam[1:2], axis=-1, keepdims=True))
                - jnp.exp(jnp.sum(lam[2:3] * lam[3:4], axis=-1, keepdims=True)) + lam_init)
    col = lax.broadcasted_iota(jnp.int32, (1, s), 1)
    k = k_ref[0]
    v = v_ref[0]
    for c in range(n_chain):
        q = q_ref[0, c * rows:(c + 1) * rows, :]
        qq = jnp.concatenate([q * m1, q * m2], axis=0)
        sc = _nt_dot(qq, k)
        row = pl.program_id(2) * tq + c * rows + lax.broadcasted_iota(jnp.int32, (rows, 1), 0)
        bias = slope * jnp.abs(row - col).astype(F32)
        s1 = sc[:rows] - bias
        s2 = sc[rows:] - bias
        e1 = jnp.exp2(s1 - jnp.max(s1, axis=-1, keepdims=True))
        e2 = jnp.exp2(s2 - jnp.max(s2, axis=-1, keepdims=True))
        r1 = 1.0 / jnp.sum(e1, axis=-1, keepdims=True)
        r2 = lam_full / jnp.sum(e2, axis=-1, keepdims=True)
        a = (e1 * r1 - e2 * r2).astype(BF16)
        o = jnp.dot(a, v, preferred_element_type=F32)
        ms = jnp.mean(o * o, axis=-1, keepdims=True)
        o = o * lax.rsqrt(ms + RMS_EPS) * g_ref[...] * (1.0 - lam_init)
        o_ref[0, c * rows:(c + 1) * rows, :] = o.astype(o_ref.dtype)


def _diff_attention(qkv, lam, subln_g, layer_idx, tq=512, n_chain=2):
    b, s, d3 = qkv.shape
    d = d3 // 3
    nh = d // LANES
    tq = min(tq, s)
    lam_init = 0.8 - 0.6 * math.exp(-0.3 * layer_idx)
    slopes = np.array([2.0 ** (-8.0 * (h + 1) / nh) for h in range(nh)], dtype=np.float32)
    slopes = jnp.asarray(np.broadcast_to(slopes[:, None, None], (nh, 1, LANES)).copy())
    return pl.pallas_call(
        functools.partial(_diff_attn_kernel, lam_init=lam_init, n_chain=n_chain),
        grid=(b, nh, s // tq),
        in_specs=[pl.BlockSpec((1, tq, LANES), lambda i, h, j: (i, j, h)),
                  pl.BlockSpec((1, s, LANES), lambda i, h, j: (i, 0, nh + h)),
                  pl.BlockSpec((1, s, LANES), lambda i, h, j: (i, 0, 2 * nh + h)),
                  pl.BlockSpec(lam.shape, lambda i, h, j: (0, 0)),
                  pl.BlockSpec((1, 1, LANES), lambda i, h, j: (h, 0, 0)),
                  pl.BlockSpec((1, LANES), lambda i, h, j: (0, 0))],
        out_specs=pl.BlockSpec((1, tq, LANES), lambda i, h, j: (i, j, h)),
        out_shape=jax.ShapeDtypeStruct((b, s, d), BF16),
        compiler_params=_params("parallel", "parallel", "arbitrary"),
        name="diff_attn",
    )(qkv, qkv, qkv, lam, slopes, subln_g.reshape(1, LANES))


def _na_kernel(q_ref, k_ref, v_ref, tbl_ref, o_ref, *, rows, kr):
    w = GRID_W
    hd = LANES // 2
    lane = lax.broadcasted_iota(jnp.int32, (1, LANES), 1)
    m0 = jnp.where(lane < hd, 1.0, 0.0).astype(BF16)
    m1 = jnp.where(lane >= hd, 1.0, 0.0).astype(BF16)

    for r in range(rows):
        rs = min(max(r - kr // 2, 0), rows - kr)
        win = rs - r + (NA_MAX_ROWS - 1)
        q = q_ref[0, r * w:(r + 1) * w, :]
        kb = k_ref[0, rs * w:(rs + kr) * w, :]
        vb = v_ref[0, rs * w:(rs + kr) * w, :]
        qq = jnp.concatenate([q * m0, q * m1], axis=0)
        sc = _nt_dot(qq, kb)
        bias = jnp.concatenate(
            [jnp.concatenate([tbl_ref[0, e, win + 2 * jj] for jj in range(kr // 2)], axis=1)
             for e in range(2)], axis=0)
        sc = sc + bias
        e = jnp.exp2(sc - jnp.max(sc, axis=-1, keepdims=True))
        rl = 1.0 / jnp.sum(e, axis=-1, keepdims=True)
        o = jnp.dot(e.astype(BF16), vb, preferred_element_type=F32) * rl
        out = jnp.where(lane < hd, o[:w], o[w:])
        o_ref[0, r * w:(r + 1) * w, :] = out.astype(o_ref.dtype)


def _na_bias_table(rpb):
    kc = NA_WIN_COLS
    col = np.arange(GRID_W)
    cs = np.clip(col - kc // 2, 0, GRID_W - kc)
    col_mask = (col[None, :] >= cs[:, None]) & (col[None, :] < cs[:, None] + kc)
    dc = np.clip(col[None, :] - col[:, None], -(kc - 1), kc - 1) + (kc - 1)
    bias = jnp.where(jnp.asarray(col_mask), rpb[:, :, dc] * LOG2E, -jnp.inf)
    tbl = jnp.concatenate([bias[:, :-1], bias[:, 1:]], axis=-1)
    h = tbl.shape[0]
    return tbl.reshape((h // 2, 2) + tbl.shape[1:])


def _neighborhood_attention(qkv, rpb):
    b, s, d3 = qkv.shape
    d = d3 // 3
    npair = d // LANES
    rows = s // GRID_W
    kr = min(NA_MAX_ROWS, rows)
    tbl = _na_bias_table(rpb.astype(F32))
    return pl.pallas_call(
        functools.partial(_na_kernel, rows=rows, kr=kr),
        grid=(b, npair),
        in_specs=[pl.BlockSpec((1, s, LANES), lambda i, p: (i, 0, p)),
                  pl.BlockSpec((1, s, LANES), lambda i, p: (i, 0, npair + p)),
                  pl.BlockSpec((1, s, LANES), lambda i, p: (i, 0, 2 * npair + p)),
                  pl.BlockSpec((1,) + tbl.shape[1:], lambda i, p: (p, 0, 0, 0, 0))],
        out_specs=pl.BlockSpec((1, s, LANES), lambda i, p: (i, 0, p)),
        out_shape=jax.ShapeDtypeStruct((b, s, d), BF16),
        compiler_params=_params("parallel", "parallel"),
        name="na_attn",
    )(qkv, qkv, qkv, tbl)


def _mla_proj_kernel(x_ref, wa_ref, gq_ref, gkv_ref, wuq_ref, wukv_ref, cos_ref, sin_ref,
                     qn_ref, qr_ref, kn_ref, kr_ref, v_ref, *, scale):
    qrank, kvrank = MLA_Q_RANK, MLA_KV_RANK
    a = jnp.dot(x_ref[...].astype(BF16), wa_ref[...], preferred_element_type=F32)
    cq = a[:, :qrank]
    cq = cq * lax.rsqrt(jnp.mean(cq * cq, axis=-1, keepdims=True) + RMS_EPS) * gq_ref[...]
    ckv = a[:, qrank:qrank + kvrank]
    ckv = ckv * lax.rsqrt(jnp.mean(ckv * ckv, axis=-1, keepdims=True) + RMS_EPS) * gkv_ref[...]
    cos = cos_ref[...]
    sin = sin_ref[...]
    o = qrank + kvrank
    kr_ref[...] = (a[:, o:o + LANES] * cos[:, :LANES] + a[:, o + LANES:o + 2 * LANES] * sin[:, :LANES]
                   ).astype(kr_ref.dtype)
    q = jnp.dot(cq.astype(BF16), wuq_ref[...], preferred_element_type=F32)
    nn = qn_ref.shape[1]
    nr = qr_ref.shape[1]
    qn_ref[...] = (q[:, :nn] * scale).astype(qn_ref.dtype)
    qr_ref[...] = ((q[:, nn:nn + nr] * cos + q[:, nn + nr:nn + 2 * nr] * sin) * scale).astype(qr_ref.dtype)
    kv = jnp.dot(ckv.astype(BF16), wukv_ref[...], preferred_element_type=F32)
    kn_ref[...] = kv[:, :nn].astype(kn_ref.dtype)
    v_ref[...] = kv[:, nn:].astype(v_ref.dtype)


def _swap_halves(t):
    half = t.shape[-1] // 2
    return jnp.concatenate([t[..., half:], t[..., :half]], axis=-1)


def _mla_projection(x, w_a, g_q, g_kv, w_uq, w_ukv, seq, tm=512):
    t, d = x.shape
    nh, nope, rope, vd = MLA_HEADS, MLA_NOPE, MLA_ROPE, MLA_V
    qrank, kvrank = MLA_Q_RANK, MLA_KV_RANK
    tm = min(tm, seq)
    rep = LANES // rope
    w_kr = w_a[:, qrank + kvrank:]
    wa_ext = jnp.concatenate([w_a[:, :qrank + kvrank], jnp.tile(w_kr, (1, rep)),
                              jnp.tile(_swap_halves(w_kr), (1, rep))], axis=1).astype(BF16)
    wq = w_uq.reshape(qrank, nh, nope + rope)
    wq_rope = wq[:, :, nope:]
    wuq_ext = jnp.concatenate([wq[:, :, :nope].reshape(qrank, nh * nope),
                               wq_rope.reshape(qrank, nh * rope),
                               _swap_halves(wq_rope).reshape(qrank, nh * rope)], axis=1).astype(BF16)
    wkv = w_ukv.reshape(kvrank, nh, nope + vd)
    wukv_p = jnp.concatenate([wkv[:, :, :nope].reshape(kvrank, nh * nope),
                              wkv[:, :, nope:].reshape(kvrank, nh * vd)], axis=1).astype(BF16)
    inv_freq = 1.0 / (ROPE_THETA ** (jnp.arange(0, rope, 2, dtype=F32) / rope))
    ang = jnp.arange(seq, dtype=F32)[:, None] * inv_freq[None, :]
    cos, sin = jnp.cos(ang), jnp.sin(ang)
    cos_t = jnp.tile(jnp.concatenate([cos, cos], axis=-1), (1, nh))
    sin_t = jnp.tile(jnp.concatenate([-sin, sin], axis=-1), (1, nh))
    nblk = seq // tm
    scale = (nope + rope) ** -0.5 * LOG2E
    na = wa_ext.shape[1]
    nq = wuq_ext.shape[1]
    nkv = wukv_p.shape[1]
    const = lambda i: (0, 0)
    tok = lambda i: (i, 0)
    return pl.pallas_call(
        functools.partial(_mla_proj_kernel, scale=scale),
        grid=(t // tm,),
        in_specs=[pl.BlockSpec((tm, d), tok),
                  pl.BlockSpec((d, na), const),
                  pl.BlockSpec((1, qrank), const),
                  pl.BlockSpec((1, kvrank), const),
                  pl.BlockSpec((qrank, nq), const),
                  pl.BlockSpec((kvrank, nkv), const),
                  pl.BlockSpec((tm, nh * rope), lambda i: (i % nblk, 0)),
                  pl.BlockSpec((tm, nh * rope), lambda i: (i % nblk, 0))],
        out_specs=[pl.BlockSpec((tm, nh * nope), tok),
                   pl.BlockSpec((tm, nh * rope), tok),
                   pl.BlockSpec((tm, nh * nope), tok),
                   pl.BlockSpec((tm, LANES), tok),
                   pl.BlockSpec((tm, nh * vd), tok)],
        out_shape=[jax.ShapeDtypeStruct((t, nh * nope), BF16),
                   jax.ShapeDtypeStruct((t, nh * rope), BF16),
                   jax.ShapeDtypeStruct((t, nh * nope), BF16),
                   jax.ShapeDtypeStruct((t, LANES), BF16),
                   jax.ShapeDtypeStruct((t, nh * vd), BF16)],
        compiler_params=_params("parallel"),
        name="mla_proj",
    )(x, wa_ext, g_q.reshape(1, qrank), g_kv.reshape(1, kvrank), wuq_ext, wukv_p, cos_t, sin_t)


def _mla_attn_kernel(qn_ref, qr_ref, kn_ref, kr_ref, v_ref, o_ref, kcat_ref):
    tq = qn_ref.shape[1]
    p = pl.program_id(1)

    @pl.when(pl.program_id(2) == 0)
    def _():
        kcat_ref[:, :LANES] = kn_ref[0]
        kcat_ref[:, LANES:] = kr_ref[0]

    lane = lax.broadcasted_iota(jnp.int32, (1, LANES), 1)
    qn = qn_ref[0]
    qr = qr_ref[0]
    kcat = kcat_ref[...]
    v = v_ref[0]
    outs = []
    for e in range(2):
        mn = jnp.where((lane >= e * MLA_NOPE) & (lane < (e + 1) * MLA_NOPE), 1.0, 0.0).astype(BF16)
        lo = (p % 2) * (2 * MLA_ROPE) + e * MLA_ROPE
        mr = jnp.where((lane >= lo) & (lane < lo + MLA_ROPE), 1.0, 0.0).astype(BF16)
        lhs = jnp.concatenate([qn * mn, qr * mr], axis=1)
        sc = _nt_dot(lhs, kcat)
        pe = jnp.exp2(sc - jnp.max(sc, axis=-1, keepdims=True))
        rl = 1.0 / jnp.sum(pe, axis=-1, keepdims=True)
        outs.append(jnp.dot(pe.astype(BF16), v, preferred_element_type=F32) * rl)
    o_ref[0] = jnp.where(lane < MLA_V, outs[0], outs[1]).astype(o_ref.dtype)


def _mla_attention(qn, qr, kn, kr, v, tq=512):
    b, s, _ = qn.shape
    npair = MLA_HEADS // 2
    tq = min(tq, s)
    return pl.pallas_call(
        _mla_attn_kernel,
        grid=(b, npair, s // tq),
        in_specs=[pl.BlockSpec((1, tq, LANES), lambda i, p, j: (i, j, p)),
                  pl.BlockSpec((1, tq, LANES), lambda i, p, j: (i, j, p // 2)),
                  pl.BlockSpec((1, s, LANES), lambda i, p, j: (i, 0, p)),
                  pl.BlockSpec((1, s, LANES), lambda i, p, j: (i, 0, 0)),
                  pl.BlockSpec((1, s, LANES), lambda i, p, j: (i, 0, p))],
        out_specs=pl.BlockSpec((1, tq, LANES), lambda i, p, j: (i, j, p)),
        out_shape=jax.ShapeDtypeStruct((b, s, npair * LANES), BF16),
        scratch_shapes=[pltpu.VMEM((s, 2 * LANES), BF16)],
        compiler_params=_params("parallel", "parallel", "arbitrary"),
        name="mla_attn",
    )(qn, qr, kn, kr, v)


def _scaled_qkv_weight(w_qkv, head_dim):
    d = w_qkv.shape[0]
    c = head_dim ** -0.5 * LOG2E
    return jnp.concatenate([w_qkv[:, :d] * c, w_qkv[:, d:]], axis=1).astype(BF16)


def kernel(x, conv_w_in, conv_w, conv_w_out, diff_w_qkv, diff_lambda, diff_subln_g, diff_w_out,
           na_w_qkv, na_rpb, na_w_out, mla_w_a, mla_g_q, mla_g_kv, mla_w_uq, mla_w_ukv, mla_w_out,
           ln1_g, ln1_b, ffn_w_gu, ffn_w_down, ln2_g, ln2_b):
    b, s, d = x.shape
    t = b * s
    xf = x.reshape(t, d)
    for i in range(DEPTH):
        m, j = i % N_MIXERS, i // N_MIXERS
        if m == 0:
            a = _conv_mixer(xf.reshape(b, s, d), conv_w_in[j].astype(BF16), conv_w[j])
            w_out = conv_w_out[j]
        elif m == 1:
            qkv = _proj(xf, _scaled_qkv_weight(diff_w_qkv[j], d // (2 * DA_HEADS)), BF16)
            a = _diff_attention(qkv.reshape(b, s, 3 * d), diff_lambda[j], diff_subln_g[j], i)
            w_out = diff_w_out[j]
        elif m == 2:
            qkv = _proj(xf, _scaled_qkv_weight(na_w_qkv[j], d // NA_HEADS), BF16)
            a = _neighborhood_attention(qkv.reshape(b, s, 3 * d), na_rpb[j])
            w_out = na_w_out[j]
        else:
            qn, qr, kn, kr, v = _mla_projection(xf, mla_w_a[j], mla_g_q[j], mla_g_kv[j],
                                                mla_w_uq[j], mla_w_ukv[j], s)
            r3 = lambda z: z.reshape(b, s, z.shape[-1])
            a = _mla_attention(r3(qn), r3(qr), r3(kn), r3(kr), r3(v))
            w_out = mla_w_out[j]
        xf = _proj_res_ln(a.reshape(t, -1), w_out.astype(BF16), xf, ln1_g[i], ln1_b[i])
        xf = _ffn(xf, ffn_w_gu[i].astype(BF16), ffn_w_down[i].astype(BF16), ln2_g[i], ln2_b[i])
    return xf.reshape(b, s, d)
```

```python
import functools
import math

import numpy as np
import jax
import jax.numpy as jnp
from jax import lax
from jax.experimental import pallas as pl
from jax.experimental.pallas import tpu as pltpu

DEPTH = 4
N_MIXERS = 4
GRID_W = 64
LN_EPS = 1e-5
RMS_EPS = 1e-6
LOG2E = math.log2(math.e)
DEEPNORM_ALPHA = (2 * DEPTH) ** 0.25
DA_HEADS = 8
NA_HEADS = 16
NA_MAX_ROWS = 8
NA_WIN_COLS = 16
MLA_HEADS = 16
MLA_Q_RANK = 256
MLA_KV_RANK = 128
MLA_NOPE = 64
MLA_ROPE = 32
MLA_V = 64
ROPE_THETA = 10000.0

LANES = 128
VMEM_LIMIT = 56 * 1024 * 1024
BF16 = jnp.bfloat16
F32 = jnp.float32


def _params(*sem):
    return pltpu.CompilerParams(dimension_semantics=sem, vmem_limit_bytes=VMEM_LIMIT)


def _layer_norm_rows(y, g, b):
    mu = jnp.mean(y, axis=-1, keepdims=True)
    yc = y - mu
    var = jnp.mean(yc * yc, axis=-1, keepdims=True)
    return yc * lax.rsqrt(var + LN_EPS) * g + b


def _nt_dot(a, b):
    return lax.dot_general(a, b, (((1,), (1,)), ((), ())), preferred_element_type=F32)


def _proj_kernel(x_ref, w_ref, o_ref, xb_ref):
    @pl.when(pl.program_id(1) == 0)
    def _():
        xb_ref[...] = x_ref[...].astype(BF16)

    o_ref[...] = jnp.dot(xb_ref[...], w_ref[...], preferred_element_type=F32).astype(o_ref.dtype)


def _proj(x, w, out_dtype, tm=1024, tn=1024):
    t, k = x.shape
    n = w.shape[1]
    tm, tn = min(tm, t), min(tn, n)
    return pl.pallas_call(
        _proj_kernel,
        grid=(t // tm, n // tn),
        in_specs=[pl.BlockSpec((tm, k), lambda i, j: (i, 0)),
                  pl.BlockSpec((k, tn), lambda i, j: (0, j))],
        out_specs=pl.BlockSpec((tm, tn), lambda i, j: (i, j)),
        out_shape=jax.ShapeDtypeStruct((t, n), out_dtype),
        scratch_shapes=[pltpu.VMEM((tm, k), BF16)],
        compiler_params=_params("parallel", "arbitrary"),
        name="proj",
    )(x, w)


def _post_mixer_kernel(a_ref, wo_ref, x_ref, g1_ref, b1_ref, wgu_ref, wd_ref, g2_ref, b2_ref, o_ref,
                       *, d_ff, tf):
    h = jnp.dot(a_ref[...], wo_ref[...], preferred_element_type=F32)
    x1 = _layer_norm_rows(DEEPNORM_ALPHA * x_ref[...] + h, g1_ref[...], b1_ref[...])
    xb = x1.astype(BF16)
    acc = jnp.zeros(x1.shape, F32)
    for c in range(d_ff // tf):
        gate = jnp.dot(xb, wgu_ref[:, c * tf:(c + 1) * tf], preferred_element_type=F32)
        up = jnp.dot(xb, wgu_ref[:, d_ff + c * tf:d_ff + (c + 1) * tf], preferred_element_type=F32)
        hid = (gate * jax.nn.sigmoid(gate) * up).astype(BF16)
        acc = acc + jnp.dot(hid, wd_ref[c * tf:(c + 1) * tf, :], preferred_element_type=F32)
    o_ref[...] = _layer_norm_rows(DEEPNORM_ALPHA * x1 + acc, g2_ref[...], b2_ref[...])


def _post_mixer(a, w_out, x, g1, b1, w_gu, w_down, g2, b2, tm=512, tf=256):
    t, d = x.shape
    k = a.shape[1]
    d_ff = w_down.shape[0]
    tm = min(tm, t)
    tok = lambda i: (i, 0)
    const = lambda i: (0, 0)
    vec = pl.BlockSpec((1, d), const)
    return pl.pallas_call(
        functools.partial(_post_mixer_kernel, d_ff=d_ff, tf=tf),
        grid=(t // tm,),
        in_specs=[pl.BlockSpec((tm, k), tok),
                  pl.BlockSpec((k, d), const),
                  pl.BlockSpec((tm, d), tok),
                  vec, vec,
                  pl.BlockSpec((d, 2 * d_ff), const),
                  pl.BlockSpec((d_ff, d), const),
                  vec, vec],
        out_specs=pl.BlockSpec((tm, d), tok),
        out_shape=jax.ShapeDtypeStruct((t, d), F32),
        compiler_params=_params("parallel"),
        name="post_mixer",
    )(a, w_out, x, g1.reshape(1, d), b1.reshape(1, d), w_gu, w_down, g2.reshape(1, d), b2.reshape(1, d))


def _conv_kernel(x_ref, wb_ref, wc_ref, wh_ref, cw_ref, o_ref, xb_ref):
    @pl.when(pl.program_id(1) == 0)
    def _():
        xb_ref[...] = x_ref[0].astype(BF16)

    xb = xb_ref[...]
    s = xb.shape[0]
    bg = jnp.dot(xb, wb_ref[...], preferred_element_type=F32)
    cg = jnp.dot(xb, wc_ref[...], preferred_element_type=F32)
    hh = jnp.dot(xb, wh_ref[...], preferred_element_type=F32)
    u = cg * hh
    row = lax.broadcasted_iota(jnp.int32, (s, 1), 0)
    prev = jnp.where(row == 0, 0.0, pltpu.roll(u, 1, axis=0))
    nxt = jnp.where(row == s - 1, 0.0, pltpu.roll(u, s - 1, axis=0))
    cw = cw_ref[...]
    y = cw[0:1, :] * prev + cw[1:2, :] * u + cw[2:3, :] * nxt
    o_ref[0] = (bg * y).astype(o_ref.dtype)


def _conv_mixer(x3, w_in, conv_w, tc=256):
    b, s, d = x3.shape
    nj = d // tc
    return pl.pallas_call(
        _conv_kernel,
        grid=(b, nj),
        in_specs=[pl.BlockSpec((1, s, d), lambda i, j: (i, 0, 0)),
                  pl.BlockSpec((d, tc), lambda i, j: (0, j)),
                  pl.BlockSpec((d, tc), lambda i, j: (0, nj + j)),
                  pl.BlockSpec((d, tc), lambda i, j: (0, 2 * nj + j)),
                  pl.BlockSpec((3, tc), lambda i, j: (0, j))],
        out_specs=pl.BlockSpec((1, s, tc), lambda i, j: (i, 0, j)),
        out_shape=jax.ShapeDtypeStruct((b, s, d), BF16),
        scratch_shapes=[pltpu.VMEM((s, d), BF16)],
        compiler_params=_params("parallel", "arbitrary"),
        name="conv_mixer",
    )(x3, w_in, w_in, w_in, conv_w)


def _diff_attn_kernel(q_ref, k_ref, v_ref, lam_ref, slope_ref, dist_ref, g_ref, o_ref, bias_ref,
                      *, lam_init, n_chain):
    tq = q_ref.shape[1]
    s = k_ref.shape[1]
    hd = LANES // 2
    rows = tq // n_chain
    nkb = s // rows

    @pl.when(pl.program_id(2) == 0)
    def _():
        bias_ref[...] = dist_ref[...] * (slope_ref[0][:, 0:1] * LOG2E)

    lane = lax.broadcasted_iota(jnp.int32, (1, LANES), 1)
    m1 = jnp.where(lane < hd, 1.0, 0.0).astype(BF16)
    m2 = jnp.where(lane >= hd, 1.0, 0.0).astype(BF16)
    lam = lam_ref[...]
    lam_full = (jnp.exp(jnp.sum(lam[0:1] * lam[1:2], axis=-1, keepdims=True))
                - jnp.exp(jnp.sum(lam[2:3] * lam[3:4], axis=-1, keepdims=True)) + lam_init)
    k = k_ref[0]

    def scores(c):
        q = q_ref[0, c * rows:(c + 1) * rows, :]
        qq = jnp.concatenate([q * m1, q * m2], axis=0)
        return _nt_dot(qq, k)

    v = v_ref[0]
    sc_next = scores(0)
    for c in range(n_chain):
        sc = sc_next
        if c + 1 < n_chain:
            sc_next = scores(c + 1)
        first = nkb - 1 - (pl.program_id(2) * n_chain + c)
        bias = jnp.concatenate([bias_ref[first + kb] for kb in range(nkb)], axis=1)
        s1 = sc[:rows] - bias
        s2 = sc[rows:] - bias
        e1 = jnp.exp2(s1 - jnp.max(s1, axis=-1, keepdims=True))
        e2 = jnp.exp2(s2 - jnp.max(s2, axis=-1, keepdims=True))
        l1 = jnp.sum(e1, axis=-1, keepdims=True)
        l2 = jnp.sum(e2, axis=-1, keepdims=True)
        a = (e1 - e2 * (lam_full * l1 / l2)).astype(BF16)
        o = jnp.dot(a, v, preferred_element_type=F32) * (1.0 / l1)
        ms = jnp.mean(o * o, axis=-1, keepdims=True)
        o = o * lax.rsqrt(ms + RMS_EPS) * g_ref[...] * (1.0 - lam_init)
        o_ref[0, c * rows:(c + 1) * rows, :] = o.astype(o_ref.dtype)


def _alibi_distance_blocks(s, rows):
    nblk = 2 * s // rows - 1
    r = np.arange(rows)[None, :, None]
    col = (np.arange(nblk)[:, None, None] * rows + np.arange(rows)[None, None, :])
    return jnp.asarray(np.abs(r - col + s - rows).astype(np.float32))


def _diff_attention(qkv, lam, subln_g, layer_idx, tq=1024, n_chain=4):
    b, s, d3 = qkv.shape
    d = d3 // 3
    nh = d // LANES
    tq = min(tq, s)
    rows = tq // n_chain
    dist = _alibi_distance_blocks(s, rows)
    lam_init = 0.8 - 0.6 * math.exp(-0.3 * layer_idx)
    slopes = np.array([2.0 ** (-8.0 * (h + 1) / nh) for h in range(nh)], dtype=np.float32)
    slopes = jnp.asarray(np.broadcast_to(slopes[:, None, None], (nh, 1, LANES)).copy())
    return pl.pallas_call(
        functools.partial(_diff_attn_kernel, lam_init=lam_init, n_chain=n_chain),
        grid=(b, nh, s // tq),
        in_specs=[pl.BlockSpec((1, tq, LANES), lambda i, h, j: (i, j, h)),
                  pl.BlockSpec((1, s, LANES), lambda i, h, j: (i, 0, nh + h)),
                  pl.BlockSpec((1, s, LANES), lambda i, h, j: (i, 0, 2 * nh + h)),
                  pl.BlockSpec(lam.shape, lambda i, h, j: (0, 0)),
                  pl.BlockSpec((1, 1, LANES), lambda i, h, j: (h, 0, 0)),
                  pl.BlockSpec(dist.shape, lambda i, h, j: (0, 0, 0)),
                  pl.BlockSpec((1, LANES), lambda i, h, j: (0, 0))],
        out_specs=pl.BlockSpec((1, tq, LANES), lambda i, h, j: (i, j, h)),
        out_shape=jax.ShapeDtypeStruct((b, s, d), BF16),
        scratch_shapes=[pltpu.VMEM(dist.shape, F32)],
        compiler_params=_params("parallel", "parallel", "arbitrary"),
        name="diff_attn",
    )(qkv, qkv, qkv, lam, slopes, dist, subln_g.reshape(1, LANES))


def _na_kernel(q_ref, k_ref, v_ref, tbl_ref, o_ref, *, rows, kr):
    w = GRID_W
    hd = LANES // 2
    lane = lax.broadcasted_iota(jnp.int32, (1, LANES), 1)
    m0 = jnp.where(lane < hd, 1.0, 0.0).astype(BF16)
    m1 = jnp.where(lane >= hd, 1.0, 0.0).astype(BF16)

    for r in range(rows):
        rs = min(max(r - kr // 2, 0), rows - kr)
        win = rs - r + (NA_MAX_ROWS - 1)
        q = q_ref[0, r * w:(r + 1) * w, :]
        kb = k_ref[0, rs * w:(rs + kr) * w, :]
        vb = v_ref[0, rs * w:(rs + kr) * w, :]
        qq = jnp.concatenate([q * m0, q * m1], axis=0)
        sc = _nt_dot(qq, kb)
        bias = jnp.concatenate(
            [jnp.concatenate([tbl_ref[0, e, win + 2 * jj] for jj in range(kr // 2)], axis=1)
             for e in range(2)], axis=0)
        sc = sc + bias
        e = jnp.exp2(sc - jnp.max(sc, axis=-1, keepdims=True))
        rl = 1.0 / jnp.sum(e, axis=-1, keepdims=True)
        o = jnp.dot(e.astype(BF16), vb, preferred_element_type=F32) * rl
        out = jnp.where(lane < hd, o[:w], o[w:])
        o_ref[0, r * w:(r + 1) * w, :] = out.astype(o_ref.dtype)


def _na_bias_table(rpb):
    kc = NA_WIN_COLS
    col = np.arange(GRID_W)
    cs = np.clip(col - kc // 2, 0, GRID_W - kc)
    col_mask = (col[None, :] >= cs[:, None]) & (col[None, :] < cs[:, None] + kc)
    dc = np.clip(col[None, :] - col[:, None], -(kc - 1), kc - 1) + (kc - 1)
    bias = jnp.where(jnp.asarray(col_mask), rpb[:, :, dc] * LOG2E, -jnp.inf)
    tbl = jnp.concatenate([bias[:, :-1], bias[:, 1:]], axis=-1)
    h = tbl.shape[0]
    return tbl.reshape((h // 2, 2) + tbl.shape[1:])


def _neighborhood_attention(qkv, rpb):
    b, s, d3 = qkv.shape
    d = d3 // 3
    npair = d // LANES
    rows = s // GRID_W
    kr = min(NA_MAX_ROWS, rows)
    tbl = _na_bias_table(rpb.astype(F32))
    return pl.pallas_call(
        functools.partial(_na_kernel, rows=rows, kr=kr),
        grid=(b, npair),
        in_specs=[pl.BlockSpec((1, s, LANES), lambda i, p: (i, 0, p)),
                  pl.BlockSpec((1, s, LANES), lambda i, p: (i, 0, npair + p)),
                  pl.BlockSpec((1, s, LANES), lambda i, p: (i, 0, 2 * npair + p)),
                  pl.BlockSpec((1,) + tbl.shape[1:], lambda i, p: (p, 0, 0, 0, 0))],
        out_specs=pl.BlockSpec((1, s, LANES), lambda i, p: (i, 0, p)),
        out_shape=jax.ShapeDtypeStruct((b, s, d), BF16),
        compiler_params=_params("parallel", "parallel"),
        name="na_attn",
    )(qkv, qkv, qkv, tbl)


def _mla_proj_kernel(x_ref, wa_ref, gq_ref, gkv_ref, wuq_ref, wukv_ref, cos_ref, sin_ref,
                     qn_ref, qr_ref, kn_ref, kr_ref, v_ref, *, scale):
    qrank, kvrank = MLA_Q_RANK, MLA_KV_RANK
    a = jnp.dot(x_ref[...].astype(BF16), wa_ref[...], preferred_element_type=F32)
    cq = a[:, :qrank]
    cq = cq * lax.rsqrt(jnp.mean(cq * cq, axis=-1, keepdims=True) + RMS_EPS) * gq_ref[...]
    ckv = a[:, qrank:qrank + kvrank]
    ckv = ckv * lax.rsqrt(jnp.mean(ckv * ckv, axis=-1, keepdims=True) + RMS_EPS) * gkv_ref[...]
    cos = cos_ref[...]
    sin = sin_ref[...]
    o = qrank + kvrank
    kr_ref[...] = (a[:, o:o + LANES] * cos[:, :LANES] + a[:, o + LANES:o + 2 * LANES] * sin[:, :LANES]
                   ).astype(kr_ref.dtype)
    q = jnp.dot(cq.astype(BF16), wuq_ref[...], preferred_element_type=F32)
    nn = qn_ref.shape[1]
    nr = qr_ref.shape[1]
    qn_ref[...] = (q[:, :nn] * scale).astype(qn_ref.dtype)
    qr_ref[...] = ((q[:, nn:nn + nr] * cos + q[:, nn + nr:nn + 2 * nr] * sin) * scale).astype(qr_ref.dtype)
    kv = jnp.dot(ckv.astype(BF16), wukv_ref[...], preferred_element_type=F32)
    kn_ref[...] = kv[:, :nn].astype(kn_ref.dtype)
    v_ref[...] = kv[:, nn:].astype(v_ref.dtype)


def _swap_halves(t):
    half = t.shape[-1] // 2
    return jnp.concatenate([t[..., half:], t[..., :half]], axis=-1)


def _mla_projection(x, w_a, g_q, g_kv, w_uq, w_ukv, seq, tm=512):
    t, d = x.shape
    nh, nope, rope, vd = MLA_HEADS, MLA_NOPE, MLA_ROPE, MLA_V
    qrank, kvrank = MLA_Q_RANK, MLA_KV_RANK
    tm = min(tm, seq)
    rep = LANES // rope
    w_kr = w_a[:, qrank + kvrank:]
    wa_ext = jnp.concatenate([w_a[:, :qrank + kvrank], jnp.tile(w_kr, (1, rep)),
                              jnp.tile(_swap_halves(w_kr), (1, rep))], axis=1).astype(BF16)
    wq = w_uq.reshape(qrank, nh, nope + rope)
    wq_rope = wq[:, :, nope:]
    wuq_ext = jnp.concatenate([wq[:, :, :nope].reshape(qrank, nh * nope),
                               wq_rope.reshape(qrank, nh * rope),
                               _swap_halves(wq_rope).reshape(qrank, nh * rope)], axis=1).astype(BF16)
    wkv = w_ukv.reshape(kvrank, nh, nope + vd)
    wukv_p = jnp.concatenate([wkv[:, :, :nope].reshape(kvrank, nh * nope),
                              wkv[:, :, nope:].reshape(kvrank, nh * vd)], axis=1).astype(BF16)
    inv_freq = 1.0 / (ROPE_THETA ** (jnp.arange(0, rope, 2, dtype=F32) / rope))
    ang = jnp.arange(seq, dtype=F32)[:, None] * inv_freq[None, :]
    cos, sin = jnp.cos(ang), jnp.sin(ang)
    cos_t = jnp.tile(jnp.concatenate([cos, cos], axis=-1), (1, nh))
    sin_t = jnp.tile(jnp.concatenate([-sin, sin], axis=-1), (1, nh))
    nblk = seq // tm
    scale = (nope + rope) ** -0.5 * LOG2E
    na = wa_ext.shape[1]
    nq = wuq_ext.shape[1]
    nkv = wukv_p.shape[1]
    const = lambda i: (0, 0)
    tok = lambda i: (i, 0)
    return pl.pallas_call(
        functools.partial(_mla_proj_kernel, scale=scale),
        grid=(t // tm,),
        in_specs=[pl.BlockSpec((tm, d), tok),
                  pl.BlockSpec((d, na), const),
                  pl.BlockSpec((1, qrank), const),
                  pl.BlockSpec((1, kvrank), const),
                  pl.BlockSpec((qrank, nq), const),
                  pl.BlockSpec((kvrank, nkv), const),
                  pl.BlockSpec((tm, nh * rope), lambda i: (i % nblk, 0)),
                  pl.BlockSpec((tm, nh * rope), lambda i: (i % nblk, 0))],
        out_specs=[pl.BlockSpec((tm, nh * nope), tok),
                   pl.BlockSpec((tm, nh * rope), tok),
                   pl.BlockSpec((tm, nh * nope), tok),
                   pl.BlockSpec((tm, LANES), tok),
                   pl.BlockSpec((tm, nh * vd), tok)],
        out_shape=[jax.ShapeDtypeStruct((t, nh * nope), BF16),
                   jax.ShapeDtypeStruct((t, nh * rope), BF16),
                   jax.ShapeDtypeStruct((t, nh * nope), BF16),
                   jax.ShapeDtypeStruct((t, LANES), BF16),
                   jax.ShapeDtypeStruct((t, nh * vd), BF16)],
        compiler_params=_params("parallel"),
        name="mla_proj",
    )(x, wa_ext, g_q.reshape(1, qrank), g_kv.reshape(1, kvrank), wuq_ext, wukv_p, cos_t, sin_t)


def _mla_attn_kernel(qn_ref, qr_ref, kn_ref, kr_ref, v_ref, o_ref, kcat_ref, *, n_sub):
    npair = kcat_ref.shape[0]
    nhead = 2 * npair
    rows = qn_ref.shape[1] // n_sub

    @pl.when(pl.program_id(2) == 0)
    def _():
        for pp in range(npair):
            kcat_ref[pp, :, :LANES] = kn_ref[0, :, pp * LANES:(pp + 1) * LANES]
            kcat_ref[pp, :, LANES:] = kr_ref[0]

    lane = lax.broadcasted_iota(jnp.int32, (1, LANES), 1)

    def scores(chain):
        sub, head = chain // nhead, chain % nhead
        pp, e = head // 2, head % 2
        qn = qn_ref[0, sub * rows:(sub + 1) * rows, pp * LANES:(pp + 1) * LANES]
        qr = qr_ref[0, sub * rows:(sub + 1) * rows, :]
        mn = jnp.where((lane >= e * MLA_NOPE) & (lane < (e + 1) * MLA_NOPE), 1.0, 0.0).astype(BF16)
        mr = jnp.where((lane >= head * MLA_ROPE) & (lane < (head + 1) * MLA_ROPE), 1.0, 0.0).astype(BF16)
        lhs = jnp.concatenate([qn * mn, qr * mr], axis=1)
        return _nt_dot(lhs, kcat_ref[pp])

    nchain = n_sub * nhead
    sc_next = scores(0)
    prev = None
    for chain in range(nchain):
        sc = sc_next
        if chain + 1 < nchain:
            sc_next = scores(chain + 1)
        sub, head = chain // nhead, chain % nhead
        pp = head // 2
        pe = jnp.exp2(sc - jnp.max(sc, axis=-1, keepdims=True))
        rl = 1.0 / jnp.sum(pe, axis=-1, keepdims=True)
        v = v_ref[0, :, pp * LANES:(pp + 1) * LANES]
        out = jnp.dot(pe.astype(BF16), v, preferred_element_type=F32) * rl
        if head % 2 == 1:
            o_ref[0, sub * rows:(sub + 1) * rows, pp * LANES:(pp + 1) * LANES] = jnp.where(
                lane < MLA_V, prev, out).astype(o_ref.dtype)
        prev = out


def _mla_attention(qn, qr, kn, kr, v, tq=1024, n_sub=2, heads_per_step=4):
    b, s, _ = qn.shape
    ngrp = MLA_HEADS // heads_per_step
    npair = heads_per_step // 2
    wn = npair * LANES
    assert heads_per_step * MLA_ROPE == LANES
    tq = min(tq, s)
    return pl.pallas_call(
        functools.partial(_mla_attn_kernel, n_sub=n_sub),
        grid=(b, ngrp, s // tq),
        in_specs=[pl.BlockSpec((1, tq, wn), lambda i, g, j: (i, j, g)),
                  pl.BlockSpec((1, tq, LANES), lambda i, g, j: (i, j, g)),
                  pl.BlockSpec((1, s, wn), lambda i, g, j: (i, 0, g)),
                  pl.BlockSpec((1, s, LANES), lambda i, g, j: (i, 0, 0)),
                  pl.BlockSpec((1, s, wn), lambda i, g, j: (i, 0, g))],
        out_specs=pl.BlockSpec((1, tq, wn), lambda i, g, j: (i, j, g)),
        out_shape=jax.ShapeDtypeStruct((b, s, ngrp * wn), BF16),
        scratch_shapes=[pltpu.VMEM((npair, s, 2 * LANES), BF16)],
        compiler_params=_params("parallel", "parallel", "arbitrary"),
        name="mla_attn",
    )(qn, qr, kn, kr, v)


def _scaled_qkv_weight(w_qkv, head_dim):
    d = w_qkv.shape[0]
    c = head_dim ** -0.5 * LOG2E
    return jnp.concatenate([w_qkv[:, :d] * c, w_qkv[:, d:]], axis=1).astype(BF16)


def kernel(x, conv_w_in, conv_w, conv_w_out, diff_w_qkv, diff_lambda, diff_subln_g, diff_w_out,
           na_w_qkv, na_rpb, na_w_out, mla_w_a, mla_g_q, mla_g_kv, mla_w_uq, mla_w_ukv, mla_w_out,
           ln1_g, ln1_b, ffn_w_gu, ffn_w_down, ln2_g, ln2_b):
    b, s, d = x.shape
    t = b * s
    xf = x.reshape(t, d)
    for i in range(DEPTH):
        m, j = i % N_MIXERS, i // N_MIXERS
        if m == 0:
            a = _conv_mixer(xf.reshape(b, s, d), conv_w_in[j].astype(BF16), conv_w[j])
            w_out = conv_w_out[j]
        elif m == 1:
            qkv = _proj(xf, _scaled_qkv_weight(diff_w_qkv[j], d // (2 * DA_HEADS)), BF16)
            a = _diff_attention(qkv.reshape(b, s, 3 * d), diff_lambda[j], diff_subln_g[j], i)
            w_out = diff_w_out[j]
        elif m == 2:
            qkv = _proj(xf, _scaled_qkv_weight(na_w_qkv[j], d // NA_HEADS), BF16)
            a = _neighborhood_attention(qkv.reshape(b, s, 3 * d), na_rpb[j])
            w_out = na_w_out[j]
        else:
            qn, qr, kn, kr, v = _mla_projection(xf, mla_w_a[j], mla_g_q[j], mla_g_kv[j],
                                                mla_w_uq[j], mla_w_ukv[j], s)
            r3 = lambda z: z.reshape(b, s, z.shape[-1])
            a = _mla_attention(r3(qn), r3(qr), r3(kn), r3(kr), r3(v))
            w_out = mla_w_out[j]
        xf = _post_mixer(a.reshape(t, -1), w_out.astype(BF16), xf, ln1_g[i], ln1_b[i],
                         ffn_w_gu[i].astype(BF16), ffn_w_down[i].astype(BF16), ln2_g[i], ln2_b[i])
    return xf.reshape(b, s, d)
```

```python
import functools
import math

import numpy as np
import jax
import jax.numpy as jnp
from jax import lax
from jax.experimental import pallas as pl
from jax.experimental.pallas import tpu as pltpu

DEPTH = 4
N_MIXERS = 4
GRID_W = 64
LN_EPS = 1e-5
RMS_EPS = 1e-6
LOG2E = math.log2(math.e)
DEEPNORM_ALPHA = (2 * DEPTH) ** 0.25
DA_HEADS = 8
NA_HEADS = 16
NA_MAX_ROWS = 8
NA_WIN_COLS = 16
MLA_HEADS = 16
MLA_Q_RANK = 256
MLA_KV_RANK = 128
MLA_NOPE = 64
MLA_ROPE = 32
MLA_V = 64
ROPE_THETA = 10000.0

LANES = 128
VMEM_LIMIT = 56 * 1024 * 1024
BF16 = jnp.bfloat16
F32 = jnp.float32


def _params(*sem):
    return pltpu.CompilerParams(dimension_semantics=sem, vmem_limit_bytes=VMEM_LIMIT)


def _layer_norm_rows(y, g, b):
    mu = jnp.mean(y, axis=-1, keepdims=True)
    yc = y - mu
    var = jnp.mean(yc * yc, axis=-1, keepdims=True)
    return yc * lax.rsqrt(var + LN_EPS) * g + b


def _nt_dot(a, b):
    return lax.dot_general(a, b, (((1,), (1,)), ((), ())), preferred_element_type=F32)


def _proj_kernel(x_ref, w_ref, o_ref, xb_ref):
    @pl.when(pl.program_id(1) == 0)
    def _():
        xb_ref[...] = x_ref[...].astype(BF16)

    o_ref[...] = jnp.dot(xb_ref[...], w_ref[...], preferred_element_type=F32).astype(o_ref.dtype)


def _proj(x, w, out_dtype, tm=1024, tn=1024):
    t, k = x.shape
    n = w.shape[1]
    tm, tn = min(tm, t), min(tn, n)
    return pl.pallas_call(
        _proj_kernel,
        grid=(t // tm, n // tn),
        in_specs=[pl.BlockSpec((tm, k), lambda i, j: (i, 0)),
                  pl.BlockSpec((k, tn), lambda i, j: (0, j))],
        out_specs=pl.BlockSpec((tm, tn), lambda i, j: (i, j)),
        out_shape=jax.ShapeDtypeStruct((t, n), out_dtype),
        scratch_shapes=[pltpu.VMEM((tm, k), BF16)],
        compiler_params=_params("parallel", "arbitrary"),
        name="proj",
    )(x, w)


def _post_mixer_kernel(a_ref, wo_ref, x_ref, g1_ref, b1_ref, wgu_ref, wd_ref, g2_ref, b2_ref, o_ref,
                       *, d_ff, tf, n_sub):
    rows = x_ref.shape[0] // n_sub
    hs = [jnp.dot(a_ref[i * rows:(i + 1) * rows, :], wo_ref[...], preferred_element_type=F32)
          for i in range(n_sub)]
    for i in range(n_sub):
        sl = slice(i * rows, (i + 1) * rows)
        x1 = _layer_norm_rows(DEEPNORM_ALPHA * x_ref[sl, :] + hs[i], g1_ref[...], b1_ref[...])
        xb = x1.astype(BF16)
        acc = jnp.zeros(x1.shape, F32)
        for c in range(d_ff // tf):
            gate = jnp.dot(xb, wgu_ref[:, c * tf:(c + 1) * tf], preferred_element_type=F32)
            up = jnp.dot(xb, wgu_ref[:, d_ff + c * tf:d_ff + (c + 1) * tf], preferred_element_type=F32)
            hid = (gate * jax.nn.sigmoid(gate) * up).astype(BF16)
            acc = acc + jnp.dot(hid, wd_ref[c * tf:(c + 1) * tf, :], preferred_element_type=F32)
        o_ref[sl, :] = _layer_norm_rows(DEEPNORM_ALPHA * x1 + acc, g2_ref[...], b2_ref[...])


def _post_mixer(a, w_out, x, g1, b1, w_gu, w_down, g2, b2, tm=1024, tf=256, n_sub=2):
    t, d = x.shape
    k = a.shape[1]
    d_ff = w_down.shape[0]
    tm = min(tm, t)
    tok = lambda i: (i, 0)
    const = lambda i: (0, 0)
    vec = pl.BlockSpec((1, d), const)
    resident = dict(pipeline_mode=pl.Buffered(1))
    return pl.pallas_call(
        functools.partial(_post_mixer_kernel, d_ff=d_ff, tf=tf, n_sub=n_sub),
        grid=(t // tm,),
        in_specs=[pl.BlockSpec((tm, k), tok),
                  pl.BlockSpec((k, d), const, **resident),
                  pl.BlockSpec((tm, d), tok),
                  vec, vec,
                  pl.BlockSpec((d, 2 * d_ff), const, **resident),
                  pl.BlockSpec((d_ff, d), const, **resident),
                  vec, vec],
        out_specs=pl.BlockSpec((tm, d), tok),
        out_shape=jax.ShapeDtypeStruct((t, d), F32),
        compiler_params=_params("parallel"),
        name="post_mixer",
    )(a, w_out, x, g1.reshape(1, d), b1.reshape(1, d), w_gu, w_down, g2.reshape(1, d), b2.reshape(1, d))


def _conv_kernel(x_ref, wb_ref, wc_ref, wh_ref, cw_ref, o_ref, xb_ref):
    @pl.when(pl.program_id(1) == 0)
    def _():
        xb_ref[...] = x_ref[0].astype(BF16)

    xb = xb_ref[...]
    s = xb.shape[0]
    bg = jnp.dot(xb, wb_ref[...], preferred_element_type=F32)
    cg = jnp.dot(xb, wc_ref[...], preferred_element_type=F32)
    hh = jnp.dot(xb, wh_ref[...], preferred_element_type=F32)
    u = cg * hh
    row = lax.broadcasted_iota(jnp.int32, (s, 1), 0)
    prev = jnp.where(row == 0, 0.0, pltpu.roll(u, 1, axis=0))
    nxt = jnp.where(row == s - 1, 0.0, pltpu.roll(u, s - 1, axis=0))
    cw = cw_ref[...]
    y = cw[0:1, :] * prev + cw[1:2, :] * u + cw[2:3, :] * nxt
    o_ref[0] = (bg * y).astype(o_ref.dtype)


def _conv_mixer(x3, w_in, conv_w, tc=256):
    b, s, d = x3.shape
    nj = d // tc
    return pl.pallas_call(
        _conv_kernel,
        grid=(b, nj),
        in_specs=[pl.BlockSpec((1, s, d), lambda i, j: (i, 0, 0)),
                  pl.BlockSpec((d, tc), lambda i, j: (0, j)),
                  pl.BlockSpec((d, tc), lambda i, j: (0, nj + j)),
                  pl.BlockSpec((d, tc), lambda i, j: (0, 2 * nj + j)),
                  pl.BlockSpec((3, tc), lambda i, j: (0, j))],
        out_specs=pl.BlockSpec((1, s, tc), lambda i, j: (i, 0, j)),
        out_shape=jax.ShapeDtypeStruct((b, s, d), BF16),
        scratch_shapes=[pltpu.VMEM((s, d), BF16)],
        compiler_params=_params("parallel", "arbitrary"),
        name="conv_mixer",
    )(x3, w_in, w_in, w_in, conv_w)


def _diff_attn_kernel(q_ref, k_ref, v_ref, lam_ref, slope_ref, dist_ref, g_ref, o_ref, bias_ref,
                      *, lam_init, n_chain):
    tq = q_ref.shape[1]
    s = k_ref.shape[1]
    hd = LANES // 2
    rows = tq // n_chain
    nkb = s // rows

    @pl.when(pl.program_id(2) == 0)
    def _():
        bias_ref[...] = dist_ref[...] * (slope_ref[0][:, 0:1] * LOG2E)

    lane = lax.broadcasted_iota(jnp.int32, (1, LANES), 1)
    m1 = jnp.where(lane < hd, 1.0, 0.0).astype(BF16)
    m2 = jnp.where(lane >= hd, 1.0, 0.0).astype(BF16)
    lam = lam_ref[...]
    lam_full = (jnp.exp(jnp.sum(lam[0:1] * lam[1:2], axis=-1, keepdims=True))
                - jnp.exp(jnp.sum(lam[2:3] * lam[3:4], axis=-1, keepdims=True)) + lam_init)
    k = k_ref[0]

    def scores(c):
        q = q_ref[0, c * rows:(c + 1) * rows, :]
        qq = jnp.concatenate([q * m1, q * m2], axis=0)
        return _nt_dot(qq, k)

    v = v_ref[0]
    sc_next = scores(0)
    for c in range(n_chain):
        sc = sc_next
        if c + 1 < n_chain:
            sc_next = scores(c + 1)
        first = nkb - 1 - (pl.program_id(2) * n_chain + c)
        bias = jnp.concatenate([bias_ref[first + kb] for kb in range(nkb)], axis=1)
        s1 = sc[:rows] - bias
        s2 = sc[rows:] - bias
        e1 = jnp.exp2(s1 - jnp.max(s1, axis=-1, keepdims=True))
        e2 = jnp.exp2(s2 - jnp.max(s2, axis=-1, keepdims=True))
        l1 = jnp.sum(e1, axis=-1, keepdims=True)
        l2 = jnp.sum(e2, axis=-1, keepdims=True)
        a = (e1 - e2 * (lam_full * l1 / l2)).astype(BF16)
        o = jnp.dot(a, v, preferred_element_type=F32) * (1.0 / l1)
        ms = jnp.mean(o * o, axis=-1, keepdims=True)
        o = o * lax.rsqrt(ms + RMS_EPS) * g_ref[...] * (1.0 - lam_init)
        o_ref[0, c * rows:(c + 1) * rows, :] = o.astype(o_ref.dtype)


def _alibi_distance_blocks(s, rows):
    nblk = 2 * s // rows - 1
    r = np.arange(rows)[None, :, None]
    col = (np.arange(nblk)[:, None, None] * rows + np.arange(rows)[None, None, :])
    return jnp.asarray(np.abs(r - col + s - rows).astype(np.float32))


def _diff_attention(qkv, lam, subln_g, layer_idx, tq=1024, n_chain=4):
    b, s, d3 = qkv.shape
    d = d3 // 3
    nh = d // LANES
    tq = min(tq, s)
    rows = tq // n_chain
    dist = _alibi_distance_blocks(s, rows)
    lam_init = 0.8 - 0.6 * math.exp(-0.3 * layer_idx)
    slopes = np.array([2.0 ** (-8.0 * (h + 1) / nh) for h in range(nh)], dtype=np.float32)
    slopes = jnp.asarray(np.broadcast_to(slopes[:, None, None], (nh, 1, LANES)).copy())
    return pl.pallas_call(
        functools.partial(_diff_attn_kernel, lam_init=lam_init, n_chain=n_chain),
        grid=(b, nh, s // tq),
        in_specs=[pl.BlockSpec((1, tq, LANES), lambda i, h, j: (i, j, h)),
                  pl.BlockSpec((1, s, LANES), lambda i, h, j: (i, 0, nh + h)),
                  pl.BlockSpec((1, s, LANES), lambda i, h, j: (i, 0, 2 * nh + h)),
                  pl.BlockSpec(lam.shape, lambda i, h, j: (0, 0)),
                  pl.BlockSpec((1, 1, LANES), lambda i, h, j: (h, 0, 0)),
                  pl.BlockSpec(dist.shape, lambda i, h, j: (0, 0, 0)),
                  pl.BlockSpec((1, LANES), lambda i, h, j: (0, 0))],
        out_specs=pl.BlockSpec((1, tq, LANES), lambda i, h, j: (i, j, h)),
        out_shape=jax.ShapeDtypeStruct((b, s, d), BF16),
        scratch_shapes=[pltpu.VMEM(dist.shape, F32)],
        compiler_params=_params("parallel", "parallel", "arbitrary"),
        name="diff_attn",
    )(qkv, qkv, qkv, lam, slopes, dist, subln_g.reshape(1, LANES))


NA_BAND = 4


def _na_kernel(q_ref, k_ref, v_ref, tbl_ref, o_ref, *, rows, kr):
    w = GRID_W
    hd = LANES // 2
    nband = rows // NA_BAND
    kwin = NA_BAND + kr
    lane = lax.broadcasted_iota(jnp.int32, (1, LANES), 1)
    m0 = jnp.where(lane < hd, 1.0, 0.0).astype(BF16)
    m1 = jnp.where(lane >= hd, 1.0, 0.0).astype(BF16)

    def window_start(band):
        return min(max(band * NA_BAND - kr // 2, 0), rows - kwin)

    def scores(band):
        q = q_ref[0, band * NA_BAND * w:(band + 1) * NA_BAND * w, :]
        qq = jnp.concatenate([q * m0, q * m1], axis=0)
        ws = window_start(band)
        return _nt_dot(qq, k_ref[0, ws * w:(ws + kwin) * w, :])

    nq = NA_BAND * w
    sc_next = scores(0)
    for band in range(nband):
        sc = sc_next
        if band + 1 < nband:
            sc_next = scores(band + 1)
        kind = 0 if band == 0 else (2 if band == nband - 1 else 1)
        sc = sc + tbl_ref[0, kind]
        e = jnp.exp2(sc - jnp.max(sc, axis=-1, keepdims=True))
        rl = 1.0 / jnp.sum(e, axis=-1, keepdims=True)
        ws = window_start(band)
        o = jnp.dot(e.astype(BF16), v_ref[0, ws * w:(ws + kwin) * w, :], preferred_element_type=F32) * rl
        out = jnp.where(lane < hd, o[:nq], o[nq:])
        o_ref[0, band * nq:(band + 1) * nq, :] = out.astype(o_ref.dtype)


def _na_bias_table(rpb, rows, kr):
    kc = NA_WIN_COLS
    w = GRID_W
    kwin = NA_BAND + kr
    col = np.arange(w)
    cs = np.clip(col - kc // 2, 0, w - kc)
    col_ok = (col[None, :] >= cs[:, None]) & (col[None, :] < cs[:, None] + kc)
    dc = np.clip(col[None, :] - col[:, None], -(kc - 1), kc - 1) + (kc - 1)
    nband = rows // NA_BAND
    dr = np.zeros((3, NA_BAND, kwin), np.int64)
    row_ok = np.zeros((3, NA_BAND, kwin), bool)
    for kind, band in enumerate((0, 1, nband - 1)):
        ws = min(max(band * NA_BAND - kr // 2, 0), rows - kwin)
        for a in range(NA_BAND):
            r = band * NA_BAND + a
            rs = min(max(r - kr // 2, 0), rows - kr)
            for kk in range(kwin):
                krow = ws + kk
                ok = rs <= krow < rs + kr
                row_ok[kind, a, kk] = ok
                dr[kind, a, kk] = krow - r + (NA_MAX_ROWS - 1) if ok else 0
    ok = row_ok[:, :, None, :, None] & col_ok[None, None, :, None, :]
    vals = rpb[:, dr[:, :, None, :, None], dc[None, None, :, None, :]] * LOG2E
    tbl = jnp.where(jnp.asarray(ok)[None], vals, -jnp.inf)
    h = tbl.shape[0]
    tbl = tbl.reshape(h // 2, 2, 3, NA_BAND * w, kwin * w)
    return jnp.transpose(tbl, (0, 2, 1, 3, 4)).reshape(h // 2, 3, 2 * NA_BAND * w, kwin * w)


def _neighborhood_attention(qkv, rpb):
    b, s, d3 = qkv.shape
    d = d3 // 3
    npair = d // LANES
    rows = s // GRID_W
    kr = min(NA_MAX_ROWS, rows)
    assert rows % NA_BAND == 0 and rows >= NA_BAND + kr
    tbl = _na_bias_table(rpb.astype(F32), rows, kr)
    return pl.pallas_call(
        functools.partial(_na_kernel, rows=rows, kr=kr),
        grid=(npair, b),
        in_specs=[pl.BlockSpec((1, s, LANES), lambda p, i: (i, 0, p)),
                  pl.BlockSpec((1, s, LANES), lambda p, i: (i, 0, npair + p)),
                  pl.BlockSpec((1, s, LANES), lambda p, i: (i, 0, 2 * npair + p)),
                  pl.BlockSpec((1,) + tbl.shape[1:], lambda p, i: (p, 0, 0, 0))],
        out_specs=pl.BlockSpec((1, s, LANES), lambda p, i: (i, 0, p)),
        out_shape=jax.ShapeDtypeStruct((b, s, d), BF16),
        compiler_params=_params("parallel", "parallel"),
        name="na_attn",
    )(qkv, qkv, qkv, tbl)


def _mla_proj_kernel(x_ref, wa_ref, gq_ref, gkv_ref, wuq_ref, wukv_ref, cos_ref, sin_ref,
                     qn_ref, qr_ref, kn_ref, kr_ref, v_ref, *, scale):
    qrank, kvrank = MLA_Q_RANK, MLA_KV_RANK
    a = jnp.dot(x_ref[...].astype(BF16), wa_ref[...], preferred_element_type=F32)
    cq = a[:, :qrank]
    cq = cq * lax.rsqrt(jnp.mean(cq * cq, axis=-1, keepdims=True) + RMS_EPS) * gq_ref[...]
    ckv = a[:, qrank:qrank + kvrank]
    ckv = ckv * lax.rsqrt(jnp.mean(ckv * ckv, axis=-1, keepdims=True) + RMS_EPS) * gkv_ref[...]
    cos = cos_ref[...]
    sin = sin_ref[...]
    o = qrank + kvrank
    kr_ref[...] = (a[:, o:o + LANES] * cos[:, :LANES] + a[:, o + LANES:o + 2 * LANES] * sin[:, :LANES]
                   ).astype(kr_ref.dtype)
    q = jnp.dot(cq.astype(BF16), wuq_ref[...], preferred_element_type=F32)
    nn = qn_ref.shape[1]
    nr = qr_ref.shape[1]
    qn_ref[...] = (q[:, :nn] * scale).astype(qn_ref.dtype)
    qr_ref[...] = ((q[:, nn:nn + nr] * cos + q[:, nn + nr:nn + 2 * nr] * sin) * scale).astype(qr_ref.dtype)
    kv = jnp.dot(ckv.astype(BF16), wukv_ref[...], preferred_element_type=F32)
    kn_ref[...] = kv[:, :nn].astype(kn_ref.dtype)
    v_ref[...] = kv[:, nn:].astype(v_ref.dtype)


def _swap_halves(t):
    half = t.shape[-1] // 2
    return jnp.concatenate([t[..., half:], t[..., :half]], axis=-1)


def _mla_projection(x, w_a, g_q, g_kv, w_uq, w_ukv, seq, tm=512):
    t, d = x.shape
    nh, nope, rope, vd = MLA_HEADS, MLA_NOPE, MLA_ROPE, MLA_V
    qrank, kvrank = MLA_Q_RANK, MLA_KV_RANK
    tm = min(tm, seq)
    rep = LANES // rope
    w_kr = w_a[:, qrank + kvrank:]
    wa_ext = jnp.concatenate([w_a[:, :qrank + kvrank], jnp.tile(w_kr, (1, rep)),
                              jnp.tile(_swap_halves(w_kr), (1, rep))], axis=1).astype(BF16)
    wq = w_uq.reshape(qrank, nh, nope + rope)
    wq_rope = wq[:, :, nope:]
    wuq_ext = jnp.concatenate([wq[:, :, :nope].reshape(qrank, nh * nope),
                               wq_rope.reshape(qrank, nh * rope),
                               _swap_halves(wq_rope).reshape(qrank, nh * rope)], axis=1).astype(BF16)
    wkv = w_ukv.reshape(kvrank, nh, nope + vd)
    wukv_p = jnp.concatenate([wkv[:, :, :nope].reshape(kvrank, nh * nope),
                              wkv[:, :, nope:].reshape(kvrank, nh * vd)], axis=1).astype(BF16)
    inv_freq = 1.0 / (ROPE_THETA ** (jnp.arange(0, rope, 2, dtype=F32) / rope))
    ang = jnp.arange(seq, dtype=F32)[:, None] * inv_freq[None, :]
    cos, sin = jnp.cos(ang), jnp.sin(ang)
    cos_t = jnp.tile(jnp.concatenate([cos, cos], axis=-1), (1, nh))
    sin_t = jnp.tile(jnp.concatenate([-sin, sin], axis=-1), (1, nh))
    nblk = seq // tm
    scale = (nope + rope) ** -0.5 * LOG2E
    na = wa_ext.shape[1]
    nq = wuq_ext.shape[1]
    nkv = wukv_p.shape[1]
    const = lambda i: (0, 0)
    tok = lambda i: (i, 0)
    return pl.pallas_call(
        functools.partial(_mla_proj_kernel, scale=scale),
        grid=(t // tm,),
        in_specs=[pl.BlockSpec((tm, d), tok),
                  pl.BlockSpec((d, na), const),
                  pl.BlockSpec((1, qrank), const),
                  pl.BlockSpec((1, kvrank), const),
                  pl.BlockSpec((qrank, nq), const),
                  pl.BlockSpec((kvrank, nkv), const),
                  pl.BlockSpec((tm, nh * rope), lambda i: (i % nblk, 0)),
                  pl.BlockSpec((tm, nh * rope), lambda i: (i % nblk, 0))],
        out_specs=[pl.BlockSpec((tm, nh * nope), tok),
                   pl.BlockSpec((tm, nh * rope), tok),
                   pl.BlockSpec((tm, nh * nope), tok),
                   pl.BlockSpec((tm, LANES), tok),
                   pl.BlockSpec((tm, nh * vd), tok)],
        out_shape=[jax.ShapeDtypeStruct((t, nh * nope), BF16),
                   jax.ShapeDtypeStruct((t, nh * rope), BF16),
                   jax.ShapeDtypeStruct((t, nh * nope), BF16),
                   jax.ShapeDtypeStruct((t, LANES), BF16),
                   jax.ShapeDtypeStruct((t, nh * vd), BF16)],
        compiler_params=_params("parallel"),
        name="mla_proj",
    )(x, wa_ext, g_q.reshape(1, qrank), g_kv.reshape(1, kvrank), wuq_ext, wukv_p, cos_t, sin_t)


def _mla_attn_kernel(qn_ref, qr_ref, kn_ref, kr_ref, v_ref, o_ref, kcat_ref, *, n_sub):
    npair = kcat_ref.shape[0]
    nhead = 2 * npair
    rows = qn_ref.shape[1] // n_sub

    @pl.when(pl.program_id(2) == 0)
    def _():
        for pp in range(npair):
            kcat_ref[pp, :, :LANES] = kn_ref[0, :, pp * LANES:(pp + 1) * LANES]
            kcat_ref[pp, :, LANES:] = kr_ref[0]

    lane = lax.broadcasted_iota(jnp.int32, (1, LANES), 1)

    def scores(chain):
        sub, head = chain // nhead, chain % nhead
        pp, e = head // 2, head % 2
        qn = qn_ref[0, sub * rows:(sub + 1) * rows, pp * LANES:(pp + 1) * LANES]
        qr = qr_ref[0, sub * rows:(sub + 1) * rows, :]
        mn = jnp.where((lane >= e * MLA_NOPE) & (lane < (e + 1) * MLA_NOPE), 1.0, 0.0).astype(BF16)
        mr = jnp.where((lane >= head * MLA_ROPE) & (lane < (head + 1) * MLA_ROPE), 1.0, 0.0).astype(BF16)
        lhs = jnp.concatenate([qn * mn, qr * mr], axis=1)
        return _nt_dot(lhs, kcat_ref[pp])

    nchain = n_sub * nhead
    sc_next = scores(0)
    prev = None
    for chain in range(nchain):
        sc = sc_next
        if chain + 1 < nchain:
            sc_next = scores(chain + 1)
        sub, head = chain // nhead, chain % nhead
        pp = head // 2
        pe = jnp.exp2(sc - jnp.max(sc, axis=-1, keepdims=True))
        rl = 1.0 / jnp.sum(pe, axis=-1, keepdims=True)
        v = v_ref[0, :, pp * LANES:(pp + 1) * LANES]
        out = jnp.dot(pe.astype(BF16), v, preferred_element_type=F32) * rl
        if head % 2 == 1:
            o_ref[0, sub * rows:(sub + 1) * rows, pp * LANES:(pp + 1) * LANES] = jnp.where(
                lane < MLA_V, prev, out).astype(o_ref.dtype)
        prev = out


def _mla_attention(qn, qr, kn, kr, v, tq=1024, n_sub=2, heads_per_step=4):
    b, s, _ = qn.shape
    ngrp = MLA_HEADS // heads_per_step
    npair = heads_per_step // 2
    wn = npair * LANES
    assert heads_per_step * MLA_ROPE == LANES
    tq = min(tq, s)
    return pl.pallas_call(
        functools.partial(_mla_attn_kernel, n_sub=n_sub),
        grid=(b, ngrp, s // tq),
        in_specs=[pl.BlockSpec((1, tq, wn), lambda i, g, j: (i, j, g)),
                  pl.BlockSpec((1, tq, LANES), lambda i, g, j: (i, j, g)),
                  pl.BlockSpec((1, s, wn), lambda i, g, j: (i, 0, g)),
                  pl.BlockSpec((1, s, LANES), lambda i, g, j: (i, 0, 0)),
                  pl.BlockSpec((1, s, wn), lambda i, g, j: (i, 0, g))],
        out_specs=pl.BlockSpec((1, tq, wn), lambda i, g, j: (i, j, g)),
        out_shape=jax.ShapeDtypeStruct((b, s, ngrp * wn), BF16),
        scratch_shapes=[pltpu.VMEM((npair, s, 2 * LANES), BF16)],
        compiler_params=_params("parallel", "parallel", "arbitrary"),
        name="mla_attn",
    )(qn, qr, kn, kr, v)


def _scaled_qkv_weight(w_qkv, head_dim):
    d = w_qkv.shape[0]
    c = head_dim ** -0.5 * LOG2E
    return jnp.concatenate([w_qkv[:, :d] * c, w_qkv[:, d:]], axis=1).astype(BF16)


def kernel(x, conv_w_in, conv_w, conv_w_out, diff_w_qkv, diff_lambda, diff_subln_g, diff_w_out,
           na_w_qkv, na_rpb, na_w_out, mla_w_a, mla_g_q, mla_g_kv, mla_w_uq, mla_w_ukv, mla_w_out,
           ln1_g, ln1_b, ffn_w_gu, ffn_w_down, ln2_g, ln2_b):
    b, s, d = x.shape
    t = b * s
    xf = x.reshape(t, d)
    for i in range(DEPTH):
        m, j = i % N_MIXERS, i // N_MIXERS
        if m == 0:
            a = _conv_mixer(xf.reshape(b, s, d), conv_w_in[j].astype(BF16), conv_w[j])
            w_out = conv_w_out[j]
        elif m == 1:
            qkv = _proj(xf, _scaled_qkv_weight(diff_w_qkv[j], d // (2 * DA_HEADS)), BF16)
            a = _diff_attention(qkv.reshape(b, s, 3 * d), diff_lambda[j], diff_subln_g[j], i)
            w_out = diff_w_out[j]
        elif m == 2:
            qkv = _proj(xf, _scaled_qkv_weight(na_w_qkv[j], d // NA_HEADS), BF16)
            a = _neighborhood_attention(qkv.reshape(b, s, 3 * d), na_rpb[j])
            w_out = na_w_out[j]
        else:
            qn, qr, kn, kr, v = _mla_projection(xf, mla_w_a[j], mla_g_q[j], mla_g_kv[j],
                                                mla_w_uq[j], mla_w_ukv[j], s)
            r3 = lambda z: z.reshape(b, s, z.shape[-1])
            a = _mla_attention(r3(qn), r3(qr), r3(kn), r3(kr), r3(v))
            w_out = mla_w_out[j]
        xf = _post_mixer(a.reshape(t, -1), w_out.astype(BF16), xf, ln1_g[i], ln1_b[i],
                         ffn_w_gu[i].astype(BF16), ffn_w_down[i].astype(BF16), ln2_g[i], ln2_b[i])
    return xf.reshape(b, s, d)
```

```python
import functools
import math

import numpy as np
import jax
import jax.numpy as jnp
from jax import lax
from jax.experimental import pallas as pl
from jax.experimental.pallas import tpu as pltpu

DEPTH = 4
N_MIXERS = 4
GRID_W = 64
LN_EPS = 1e-5
RMS_EPS = 1e-6
LOG2E = math.log2(math.e)
DEEPNORM_ALPHA = (2 * DEPTH) ** 0.25
DA_HEADS = 8
NA_HEADS = 16
NA_MAX_ROWS = 8
NA_WIN_COLS = 16
MLA_HEADS = 16
MLA_Q_RANK = 256
MLA_KV_RANK = 128
MLA_NOPE = 64
MLA_ROPE = 32
MLA_V = 64
ROPE_THETA = 10000.0

LANES = 128
VMEM_LIMIT = 56 * 1024 * 1024
BF16 = jnp.bfloat16
F32 = jnp.float32


def _params(*sem):
    return pltpu.CompilerParams(dimension_semantics=sem, vmem_limit_bytes=VMEM_LIMIT)


def _layer_norm_rows(y, g, b):
    mu = jnp.mean(y, axis=-1, keepdims=True)
    yc = y - mu
    var = jnp.mean(yc * yc, axis=-1, keepdims=True)
    return yc * lax.rsqrt(var + LN_EPS) * g + b


def _nt_dot(a, b):
    return lax.dot_general(a, b, (((1,), (1,)), ((), ())), preferred_element_type=F32)


def _proj_kernel(x_ref, w_ref, o_ref):
    o_ref[...] = jnp.dot(x_ref[...].astype(BF16), w_ref[...], preferred_element_type=F32).astype(o_ref.dtype)


def _proj(x, w, out_dtype, tm=1024):
    t, k = x.shape
    n = w.shape[1]
    tm = min(tm, t)
    return pl.pallas_call(
        _proj_kernel,
        grid=(t // tm,),
        in_specs=[pl.BlockSpec((tm, k), lambda i: (i, 0)),
                  pl.BlockSpec((k, n), lambda i: (0, 0), pipeline_mode=pl.Buffered(1))],
        out_specs=pl.BlockSpec((tm, n), lambda i: (i, 0)),
        out_shape=jax.ShapeDtypeStruct((t, n), out_dtype),
        compiler_params=_params("parallel"),
        name="proj",
    )(x, w)


def _post_mixer_kernel(a_ref, wo_ref, x_ref, g1_ref, b1_ref, wgu_ref, wd_ref, g2_ref, b2_ref, o_ref,
                       *, d_ff, tf, n_sub):
    rows = x_ref.shape[0] // n_sub
    hs = [jnp.dot(a_ref[i * rows:(i + 1) * rows, :], wo_ref[...], preferred_element_type=F32)
          for i in range(n_sub)]
    for i in range(n_sub):
        sl = slice(i * rows, (i + 1) * rows)
        x1 = _layer_norm_rows(DEEPNORM_ALPHA * x_ref[sl, :] + hs[i], g1_ref[...], b1_ref[...])
        xb = x1.astype(BF16)
        acc = jnp.zeros(x1.shape, F32)
        for c in range(d_ff // tf):
            gate = jnp.dot(xb, wgu_ref[:, c * tf:(c + 1) * tf], preferred_element_type=F32)
            up = jnp.dot(xb, wgu_ref[:, d_ff + c * tf:d_ff + (c + 1) * tf], preferred_element_type=F32)
            hid = (gate * jax.nn.sigmoid(gate) * up).astype(BF16)
            acc = acc + jnp.dot(hid, wd_ref[c * tf:(c + 1) * tf, :], preferred_element_type=F32)
        o_ref[sl, :] = _layer_norm_rows(DEEPNORM_ALPHA * x1 + acc, g2_ref[...], b2_ref[...])


def _post_mixer(a, w_out, x, g1, b1, w_gu, w_down, g2, b2, tm=1024, tf=256, n_sub=2):
    t, d = x.shape
    k = a.shape[1]
    d_ff = w_down.shape[0]
    tm = min(tm, t)
    tok = lambda i: (i, 0)
    const = lambda i: (0, 0)
    vec = pl.BlockSpec((1, d), const)
    resident = dict(pipeline_mode=pl.Buffered(1))
    return pl.pallas_call(
        functools.partial(_post_mixer_kernel, d_ff=d_ff, tf=tf, n_sub=n_sub),
        grid=(t // tm,),
        in_specs=[pl.BlockSpec((tm, k), tok),
                  pl.BlockSpec((k, d), const, **resident),
                  pl.BlockSpec((tm, d), tok),
                  vec, vec,
                  pl.BlockSpec((d, 2 * d_ff), const, **resident),
                  pl.BlockSpec((d_ff, d), const, **resident),
                  vec, vec],
        out_specs=pl.BlockSpec((tm, d), tok),
        out_shape=jax.ShapeDtypeStruct((t, d), F32),
        compiler_params=_params("parallel"),
        name="post_mixer",
    )(a, w_out, x, g1.reshape(1, d), b1.reshape(1, d), w_gu, w_down, g2.reshape(1, d), b2.reshape(1, d))


def _conv_kernel(x_ref, wb_ref, wc_ref, wh_ref, cw_ref, o_ref, xb_ref):
    @pl.when(pl.program_id(1) == 0)
    def _():
        xb_ref[...] = x_ref[0].astype(BF16)

    xb = xb_ref[...]
    s = xb.shape[0]
    bg = jnp.dot(xb, wb_ref[...], preferred_element_type=F32)
    cg = jnp.dot(xb, wc_ref[...], preferred_element_type=F32)
    hh = jnp.dot(xb, wh_ref[...], preferred_element_type=F32)
    u = cg * hh
    row = lax.broadcasted_iota(jnp.int32, (s, 1), 0)
    prev = jnp.where(row == 0, 0.0, pltpu.roll(u, 1, axis=0))
    nxt = jnp.where(row == s - 1, 0.0, pltpu.roll(u, s - 1, axis=0))
    cw = cw_ref[...]
    y = cw[0:1, :] * prev + cw[1:2, :] * u + cw[2:3, :] * nxt
    o_ref[0] = (bg * y).astype(o_ref.dtype)


def _conv_mixer(x3, w_in, conv_w, tc=256):
    b, s, d = x3.shape
    nj = d // tc
    return pl.pallas_call(
        _conv_kernel,
        grid=(b, nj),
        in_specs=[pl.BlockSpec((1, s, d), lambda i, j: (i, 0, 0)),
                  pl.BlockSpec((d, tc), lambda i, j: (0, j)),
                  pl.BlockSpec((d, tc), lambda i, j: (0, nj + j)),
                  pl.BlockSpec((d, tc), lambda i, j: (0, 2 * nj + j)),
                  pl.BlockSpec((3, tc), lambda i, j: (0, j))],
        out_specs=pl.BlockSpec((1, s, tc), lambda i, j: (i, 0, j)),
        out_shape=jax.ShapeDtypeStruct((b, s, d), BF16),
        scratch_shapes=[pltpu.VMEM((s, d), BF16)],
        compiler_params=_params("parallel", "arbitrary"),
        name="conv_mixer",
    )(x3, w_in, w_in, w_in, conv_w)


def _diff_attn_kernel(q_ref, k_ref, v_ref, lam_ref, slope_ref, dist_ref, g_ref, o_ref, bias_ref,
                      *, lam_init, n_chain):
    tq = q_ref.shape[1]
    s = k_ref.shape[1]
    hd = LANES // 2
    rows = tq // n_chain
    nkb = s // rows

    @pl.when(pl.program_id(2) == 0)
    def _():
        bias_ref[...] = dist_ref[...] * (slope_ref[0][:, 0:1] * LOG2E)

    lane = lax.broadcasted_iota(jnp.int32, (1, LANES), 1)
    m1 = jnp.where(lane < hd, 1.0, 0.0).astype(BF16)
    m2 = jnp.where(lane >= hd, 1.0, 0.0).astype(BF16)
    lam = lam_ref[...]
    lam_full = (jnp.exp(jnp.sum(lam[0:1] * lam[1:2], axis=-1, keepdims=True))
                - jnp.exp(jnp.sum(lam[2:3] * lam[3:4], axis=-1, keepdims=True)) + lam_init)
    k = k_ref[0]

    def scores(c):
        q = q_ref[0, c * rows:(c + 1) * rows, :]
        return _nt_dot(q * m1, k), _nt_dot(q * m2, k)

    v = v_ref[0]
    sc_next = scores(0)
    for c in range(n_chain):
        sc = sc_next
        if c + 1 < n_chain:
            sc_next = scores(c + 1)
        first = nkb - 1 - (pl.program_id(2) * n_chain + c)
        bias = jnp.concatenate([bias_ref[first + kb] for kb in range(nkb)], axis=1)
        s1 = sc[0] - bias
        s2 = sc[1] - bias
        e1 = jnp.exp2(s1 - jnp.max(s1, axis=-1, keepdims=True))
        e2 = jnp.exp2(s2 - jnp.max(s2, axis=-1, keepdims=True))
        l1 = jnp.sum(e1, axis=-1, keepdims=True)
        l2 = jnp.sum(e2, axis=-1, keepdims=True)
        a = (e1 - e2 * (lam_full * l1 / l2)).astype(BF16)
        o = jnp.dot(a, v, preferred_element_type=F32) * (1.0 / l1)
        ms = jnp.mean(o * o, axis=-1, keepdims=True)
        o = o * lax.rsqrt(ms + RMS_EPS) * g_ref[...] * (1.0 - lam_init)
        o_ref[0, c * rows:(c + 1) * rows, :] = o.astype(o_ref.dtype)


def _alibi_distance_blocks(s, rows):
    nblk = 2 * s // rows - 1
    r = np.arange(rows)[None, :, None]
    col = (np.arange(nblk)[:, None, None] * rows + np.arange(rows)[None, None, :])
    return jnp.asarray(np.abs(r - col + s - rows).astype(np.float32))


def _diff_attention(qkv, lam, subln_g, layer_idx, tq=1024, n_chain=4):
    b, s, d3 = qkv.shape
    d = d3 // 3
    nh = d // LANES
    tq = min(tq, s)
    rows = tq // n_chain
    dist = _alibi_distance_blocks(s, rows)
    lam_init = 0.8 - 0.6 * math.exp(-0.3 * layer_idx)
    slopes = np.array([2.0 ** (-8.0 * (h + 1) / nh) for h in range(nh)], dtype=np.float32)
    slopes = jnp.asarray(np.broadcast_to(slopes[:, None, None], (nh, 1, LANES)).copy())
    return pl.pallas_call(
        functools.partial(_diff_attn_kernel, lam_init=lam_init, n_chain=n_chain),
        grid=(b, nh, s // tq),
        in_specs=[pl.BlockSpec((1, tq, LANES), lambda i, h, j: (i, j, h)),
                  pl.BlockSpec((1, s, LANES), lambda i, h, j: (i, 0, nh + h)),
                  pl.BlockSpec((1, s, LANES), lambda i, h, j: (i, 0, 2 * nh + h)),
                  pl.BlockSpec(lam.shape, lambda i, h, j: (0, 0)),
                  pl.BlockSpec((1, 1, LANES), lambda i, h, j: (h, 0, 0)),
                  pl.BlockSpec(dist.shape, lambda i, h, j: (0, 0, 0)),
                  pl.BlockSpec((1, LANES), lambda i, h, j: (0, 0))],
        out_specs=pl.BlockSpec((1, tq, LANES), lambda i, h, j: (i, j, h)),
        out_shape=jax.ShapeDtypeStruct((b, s, d), BF16),
        scratch_shapes=[pltpu.VMEM(dist.shape, F32)],
        compiler_params=_params("parallel", "parallel", "arbitrary"),
        name="diff_attn",
    )(qkv, qkv, qkv, lam, slopes, dist, subln_g.reshape(1, LANES))


NA_BAND = 4


def _na_kernel(q_ref, k_ref, v_ref, tbl_ref, o_ref, *, rows, kr):
    w = GRID_W
    hd = LANES // 2
    nband = rows // NA_BAND
    kwin = NA_BAND + kr
    lane = lax.broadcasted_iota(jnp.int32, (1, LANES), 1)
    m0 = jnp.where(lane < hd, 1.0, 0.0).astype(BF16)
    m1 = jnp.where(lane >= hd, 1.0, 0.0).astype(BF16)

    def window_start(band):
        return min(max(band * NA_BAND - kr // 2, 0), rows - kwin)

    def scores(band):
        q = q_ref[0, band * NA_BAND * w:(band + 1) * NA_BAND * w, :]
        qq = jnp.concatenate([q * m0, q * m1], axis=0)
        ws = window_start(band)
        return _nt_dot(qq, k_ref[0, ws * w:(ws + kwin) * w, :])

    nq = NA_BAND * w
    sc_next = scores(0)
    for band in range(nband):
        sc = sc_next
        if band + 1 < nband:
            sc_next = scores(band + 1)
        kind = 0 if band == 0 else (2 if band == nband - 1 else 1)
        sc = sc + tbl_ref[0, kind]
        e = jnp.exp2(sc - jnp.max(sc, axis=-1, keepdims=True))
        rl = 1.0 / jnp.sum(e, axis=-1, keepdims=True)
        ws = window_start(band)
        o = jnp.dot(e.astype(BF16), v_ref[0, ws * w:(ws + kwin) * w, :], preferred_element_type=F32) * rl
        out = jnp.where(lane < hd, o[:nq], o[nq:])
        o_ref[0, band * nq:(band + 1) * nq, :] = out.astype(o_ref.dtype)


def _na_bias_table(rpb, rows, kr):
    kc = NA_WIN_COLS
    w = GRID_W
    kwin = NA_BAND + kr
    nband = rows // NA_BAND
    col = np.arange(w)
    cs = np.clip(col - kc // 2, 0, w - kc)
    col_ok = (col[None, :] >= cs[:, None]) & (col[None, :] < cs[:, None] + kc)
    dc = np.clip(col[None, :] - col[:, None], -(kc - 1), kc - 1) + (kc - 1)
    bias = jnp.where(jnp.asarray(col_ok), rpb[:, :, dc] * LOG2E, -jnp.inf)

    def build(bias_h):
        masked = jnp.full(bias_h[:, 0].shape, -jnp.inf, F32)
        kinds = []
        for band in (0, 1, nband - 1):
            ws = min(max(band * NA_BAND - kr // 2, 0), rows - kwin)
            row_blocks = []
            for a in range(NA_BAND):
                r = band * NA_BAND + a
                rs = min(max(r - kr // 2, 0), rows - kr)
                cols = [bias_h[:, ws + kk - r + (NA_MAX_ROWS - 1)] if rs <= ws + kk < rs + kr else masked
                        for kk in range(kwin)]
                row_blocks.append(jnp.concatenate(cols, axis=-1))
            kinds.append(jnp.concatenate(row_blocks, axis=-2))
        return jnp.stack(kinds, axis=1)

    return jnp.concatenate([build(bias[0::2]), build(bias[1::2])], axis=2)


def _neighborhood_attention(qkv, rpb):
    b, s, d3 = qkv.shape
    d = d3 // 3
    npair = d // LANES
    rows = s // GRID_W
    kr = min(NA_MAX_ROWS, rows)
    assert rows % NA_BAND == 0 and rows >= NA_BAND + kr
    tbl = _na_bias_table(rpb.astype(F32), rows, kr)
    return pl.pallas_call(
        functools.partial(_na_kernel, rows=rows, kr=kr),
        grid=(npair, b),
        in_specs=[pl.BlockSpec((1, s, LANES), lambda p, i: (i, 0, p)),
                  pl.BlockSpec((1, s, LANES), lambda p, i: (i, 0, npair + p)),
                  pl.BlockSpec((1, s, LANES), lambda p, i: (i, 0, 2 * npair + p)),
                  pl.BlockSpec((1,) + tbl.shape[1:], lambda p, i: (p, 0, 0, 0))],
        out_specs=pl.BlockSpec((1, s, LANES), lambda p, i: (i, 0, p)),
        out_shape=jax.ShapeDtypeStruct((b, s, d), BF16),
        compiler_params=_params("parallel", "parallel"),
        name="na_attn",
    )(qkv, qkv, qkv, tbl)


def _mla_proj_kernel(x_ref, wa_ref, gq_ref, gkv_ref, wuq_ref, wukv_ref, cos_ref, sin_ref,
                     qn_ref, qr_ref, kn_ref, kr_ref, v_ref, *, scale):
    qrank, kvrank = MLA_Q_RANK, MLA_KV_RANK
    a = jnp.dot(x_ref[...].astype(BF16), wa_ref[...], preferred_element_type=F32)
    cq = a[:, :qrank]
    cq = cq * lax.rsqrt(jnp.mean(cq * cq, axis=-1, keepdims=True) + RMS_EPS) * gq_ref[...]
    ckv = a[:, qrank:qrank + kvrank]
    ckv = ckv * lax.rsqrt(jnp.mean(ckv * ckv, axis=-1, keepdims=True) + RMS_EPS) * gkv_ref[...]
    cos = cos_ref[...]
    sin = sin_ref[...]
    o = qrank + kvrank
    kr_ref[...] = (a[:, o:o + LANES] * cos[:, :LANES] + a[:, o + LANES:o + 2 * LANES] * sin[:, :LANES]
                   ).astype(kr_ref.dtype)
    q = jnp.dot(cq.astype(BF16), wuq_ref[...], preferred_element_type=F32)
    nn = qn_ref.shape[1]
    nr = qr_ref.shape[1]
    qn_ref[...] = (q[:, :nn] * scale).astype(qn_ref.dtype)
    qr_ref[...] = ((q[:, nn:nn + nr] * cos + q[:, nn + nr:nn + 2 * nr] * sin) * scale).astype(qr_ref.dtype)
    kv = jnp.dot(ckv.astype(BF16), wukv_ref[...], preferred_element_type=F32)
    kn_ref[...] = kv[:, :nn].astype(kn_ref.dtype)
    v_ref[...] = kv[:, nn:].astype(v_ref.dtype)


def _swap_halves(t):
    half = t.shape[-1] // 2
    return jnp.concatenate([t[..., half:], t[..., :half]], axis=-1)


def _mla_projection(x, w_a, g_q, g_kv, w_uq, w_ukv, seq, tm=512):
    t, d = x.shape
    nh, nope, rope, vd = MLA_HEADS, MLA_NOPE, MLA_ROPE, MLA_V
    qrank, kvrank = MLA_Q_RANK, MLA_KV_RANK
    tm = min(tm, seq)
    rep = LANES // rope
    w_kr = w_a[:, qrank + kvrank:]
    wa_ext = jnp.concatenate([w_a[:, :qrank + kvrank], jnp.tile(w_kr, (1, rep)),
                              jnp.tile(_swap_halves(w_kr), (1, rep))], axis=1).astype(BF16)
    wq = w_uq.reshape(qrank, nh, nope + rope)
    wq_rope = wq[:, :, nope:]
    wuq_ext = jnp.concatenate([wq[:, :, :nope].reshape(qrank, nh * nope),
                               wq_rope.reshape(qrank, nh * rope),
                               _swap_halves(wq_rope).reshape(qrank, nh * rope)], axis=1).astype(BF16)
    wkv = w_ukv.reshape(kvrank, nh, nope + vd)
    wukv_p = jnp.concatenate([wkv[:, :, :nope].reshape(kvrank, nh * nope),
                              wkv[:, :, nope:].reshape(kvrank, nh * vd)], axis=1).astype(BF16)
    inv_freq = 1.0 / (ROPE_THETA ** (jnp.arange(0, rope, 2, dtype=F32) / rope))
    ang = jnp.arange(seq, dtype=F32)[:, None] * inv_freq[None, :]
    cos, sin = jnp.cos(ang), jnp.sin(ang)
    cos_t = jnp.tile(jnp.concatenate([cos, cos], axis=-1), (1, nh))
    sin_t = jnp.tile(jnp.concatenate([-sin, sin], axis=-1), (1, nh))
    nblk = seq // tm
    scale = (nope + rope) ** -0.5 * LOG2E
    na = wa_ext.shape[1]
    nq = wuq_ext.shape[1]
    nkv = wukv_p.shape[1]
    const = lambda i: (0, 0)
    tok = lambda i: (i, 0)
    return pl.pallas_call(
        functools.partial(_mla_proj_kernel, scale=scale),
        grid=(t // tm,),
        in_specs=[pl.BlockSpec((tm, d), tok),
                  pl.BlockSpec((d, na), const),
                  pl.BlockSpec((1, qrank), const),
                  pl.BlockSpec((1, kvrank), const),
                  pl.BlockSpec((qrank, nq), const),
                  pl.BlockSpec((kvrank, nkv), const),
                  pl.BlockSpec((tm, nh * rope), lambda i: (i % nblk, 0)),
                  pl.BlockSpec((tm, nh * rope), lambda i: (i % nblk, 0))],
        out_specs=[pl.BlockSpec((tm, nh * nope), tok),
                   pl.BlockSpec((tm, nh * rope), tok),
                   pl.BlockSpec((tm, nh * nope), tok),
                   pl.BlockSpec((tm, LANES), tok),
                   pl.BlockSpec((tm, nh * vd), tok)],
        out_shape=[jax.ShapeDtypeStruct((t, nh * nope), BF16),
                   jax.ShapeDtypeStruct((t, nh * rope), BF16),
                   jax.ShapeDtypeStruct((t, nh * nope), BF16),
                   jax.ShapeDtypeStruct((t, LANES), BF16),
                   jax.ShapeDtypeStruct((t, nh * vd), BF16)],
        compiler_params=_params("parallel"),
        name="mla_proj",
    )(x, wa_ext, g_q.reshape(1, qrank), g_kv.reshape(1, kvrank), wuq_ext, wukv_p, cos_t, sin_t)


def _mla_attn_kernel(qn_ref, qr_ref, kn_ref, kr_ref, v_ref, o_ref, kcat_ref, *, n_sub):
    npair = kcat_ref.shape[0]
    nhead = 2 * npair
    rows = qn_ref.shape[1] // n_sub

    @pl.when(pl.program_id(2) == 0)
    def _():
        for pp in range(npair):
            kcat_ref[pp, :, :LANES] = kn_ref[0, :, pp * LANES:(pp + 1) * LANES]
            kcat_ref[pp, :, LANES:] = kr_ref[0]

    lane = lax.broadcasted_iota(jnp.int32, (1, LANES), 1)

    def scores(chain):
        sub, head = chain // nhead, chain % nhead
        pp, e = head // 2, head % 2
        qn = qn_ref[0, sub * rows:(sub + 1) * rows, pp * LANES:(pp + 1) * LANES]
        qr = qr_ref[0, sub * rows:(sub + 1) * rows, :]
        mn = jnp.where((lane >= e * MLA_NOPE) & (lane < (e + 1) * MLA_NOPE), 1.0, 0.0).astype(BF16)
        mr = jnp.where((lane >= head * MLA_ROPE) & (lane < (head + 1) * MLA_ROPE), 1.0, 0.0).astype(BF16)
        lhs = jnp.concatenate([qn * mn, qr * mr], axis=1)
        return _nt_dot(lhs, kcat_ref[pp])

    nchain = n_sub * nhead
    sc_next = scores(0)
    prev = None
    for chain in range(nchain):
        sc = sc_next
        if chain + 1 < nchain:
            sc_next = scores(chain + 1)
        sub, head = chain // nhead, chain % nhead
        pp = head // 2
        pe = jnp.exp2(sc - jnp.max(sc, axis=-1, keepdims=True))
        rl = 1.0 / jnp.sum(pe, axis=-1, keepdims=True)
        v = v_ref[0, :, pp * LANES:(pp + 1) * LANES]
        out = jnp.dot(pe.astype(BF16), v, preferred_element_type=F32) * rl
        if head % 2 == 1:
            o_ref[0, sub * rows:(sub + 1) * rows, pp * LANES:(pp + 1) * LANES] = jnp.where(
                lane < MLA_V, prev, out).astype(o_ref.dtype)
        prev = out


def _mla_attention(qn, qr, kn, kr, v, tq=1024, n_sub=2, heads_per_step=4):
    b, s, _ = qn.shape
    ngrp = MLA_HEADS // heads_per_step
    npair = heads_per_step // 2
    wn = npair * LANES
    assert heads_per_step * MLA_ROPE == LANES
    tq = min(tq, s)
    return pl.pallas_call(
        functools.partial(_mla_attn_kernel, n_sub=n_sub),
        grid=(b, ngrp, s // tq),
        in_specs=[pl.BlockSpec((1, tq, wn), lambda i, g, j: (i, j, g)),
                  pl.BlockSpec((1, tq, LANES), lambda i, g, j: (i, j, g)),
                  pl.BlockSpec((1, s, wn), lambda i, g, j: (i, 0, g)),
                  pl.BlockSpec((1, s, LANES), lambda i, g, j: (i, 0, 0)),
                  pl.BlockSpec((1, s, wn), lambda i, g, j: (i, 0, g))],
        out_specs=pl.BlockSpec((1, tq, wn), lambda i, g, j: (i, j, g)),
        out_shape=jax.ShapeDtypeStruct((b, s, ngrp * wn), BF16),
        scratch_shapes=[pltpu.VMEM((npair, s, 2 * LANES), BF16)],
        compiler_params=_params("parallel", "parallel", "arbitrary"),
        name="mla_attn",
    )(qn, qr, kn, kr, v)


def _scaled_qkv_weight(w_qkv, head_dim):
    d = w_qkv.shape[0]
    c = head_dim ** -0.5 * LOG2E
    return jnp.concatenate([w_qkv[:, :d] * c, w_qkv[:, d:]], axis=1).astype(BF16)


def kernel(x, conv_w_in, conv_w, conv_w_out, diff_w_qkv, diff_lambda, diff_subln_g, diff_w_out,
           na_w_qkv, na_rpb, na_w_out, mla_w_a, mla_g_q, mla_g_kv, mla_w_uq, mla_w_ukv, mla_w_out,
           ln1_g, ln1_b, ffn_w_gu, ffn_w_down, ln2_g, ln2_b):
    b, s, d = x.shape
    t = b * s
    xf = x.reshape(t, d)
    for i in range(DEPTH):
        m, j = i % N_MIXERS, i // N_MIXERS
        if m == 0:
            a = _conv_mixer(xf.reshape(b, s, d), conv_w_in[j].astype(BF16), conv_w[j])
            w_out = conv_w_out[j]
        elif m == 1:
            qkv = _proj(xf, _scaled_qkv_weight(diff_w_qkv[j], d // (2 * DA_HEADS)), BF16)
            a = _diff_attention(qkv.reshape(b, s, 3 * d), diff_lambda[j], diff_subln_g[j], i)
            w_out = diff_w_out[j]
        elif m == 2:
            qkv = _proj(xf, _scaled_qkv_weight(na_w_qkv[j], d // NA_HEADS), BF16)
            a = _neighborhood_attention(qkv.reshape(b, s, 3 * d), na_rpb[j])
            w_out = na_w_out[j]
        else:
            qn, qr, kn, kr, v = _mla_projection(xf, mla_w_a[j], mla_g_q[j], mla_g_kv[j],
                                                mla_w_uq[j], mla_w_ukv[j], s)
            r3 = lambda z: z.reshape(b, s, z.shape[-1])
            a = _mla_attention(r3(qn), r3(qr), r3(kn), r3(kr), r3(v))
            w_out = mla_w_out[j]
        xf = _post_mixer(a.reshape(t, -1), w_out.astype(BF16), xf, ln1_g[i], ln1_b[i],
                         ffn_w_gu[i].astype(BF16), ffn_w_down[i].astype(BF16), ln2_g[i], ln2_b[i])
    return xf.reshape(b, s, d)
```

```python
import functools
import math

import numpy as np
import jax
import jax.numpy as jnp
from jax import lax
from jax.experimental import pallas as pl
from jax.experimental.pallas import tpu as pltpu

DEPTH = 4
N_MIXERS = 4
GRID_W = 64
LN_EPS = 1e-5
RMS_EPS = 1e-6
LOG2E = math.log2(math.e)
DEEPNORM_ALPHA = (2 * DEPTH) ** 0.25
DA_HEADS = 8
NA_HEADS = 16
NA_MAX_ROWS = 8
NA_WIN_COLS = 16
MLA_HEADS = 16
MLA_Q_RANK = 256
MLA_KV_RANK = 128
MLA_NOPE = 64
MLA_ROPE = 32
MLA_V = 64
ROPE_THETA = 10000.0

LANES = 128
VMEM_LIMIT = 56 * 1024 * 1024
BF16 = jnp.bfloat16
F32 = jnp.float32


def _params(*sem):
    return pltpu.CompilerParams(dimension_semantics=sem, vmem_limit_bytes=VMEM_LIMIT)


def _layer_norm_rows(y, g, b):
    mu = jnp.mean(y, axis=-1, keepdims=True)
    yc = y - mu
    var = jnp.mean(yc * yc, axis=-1, keepdims=True)
    return yc * lax.rsqrt(var + LN_EPS) * g + b


def _nt_dot(a, b):
    return lax.dot_general(a, b, (((1,), (1,)), ((), ())), preferred_element_type=F32)


def _proj_kernel(x_ref, w_ref, o_ref):
    o_ref[...] = jnp.dot(x_ref[...].astype(BF16), w_ref[...], preferred_element_type=F32).astype(o_ref.dtype)


def _proj(x, w, out_dtype, tm=1024):
    t, k = x.shape
    n = w.shape[1]
    tm = min(tm, t)
    return pl.pallas_call(
        _proj_kernel,
        grid=(t // tm,),
        in_specs=[pl.BlockSpec((tm, k), lambda i: (i, 0)),
                  pl.BlockSpec((k, n), lambda i: (0, 0), pipeline_mode=pl.Buffered(1))],
        out_specs=pl.BlockSpec((tm, n), lambda i: (i, 0)),
        out_shape=jax.ShapeDtypeStruct((t, n), out_dtype),
        compiler_params=_params("parallel"),
        name="proj",
    )(x, w)


def _post_mixer_kernel(a_ref, wo_ref, x_ref, g1_ref, b1_ref, wgu_ref, wd_ref, g2_ref, b2_ref, o_ref,
                       *, d_ff, tf, n_sub):
    rows = x_ref.shape[0] // n_sub
    hs = [jnp.dot(a_ref[i * rows:(i + 1) * rows, :], wo_ref[...], preferred_element_type=F32)
          for i in range(n_sub)]
    for i in range(n_sub):
        sl = slice(i * rows, (i + 1) * rows)
        x1 = _layer_norm_rows(DEEPNORM_ALPHA * x_ref[sl, :] + hs[i], g1_ref[...], b1_ref[...])
        xb = x1.astype(BF16)
        acc = jnp.zeros(x1.shape, F32)
        for c in range(d_ff // tf):
            gate = jnp.dot(xb, wgu_ref[:, c * tf:(c + 1) * tf], preferred_element_type=F32)
            up = jnp.dot(xb, wgu_ref[:, d_ff + c * tf:d_ff + (c + 1) * tf], preferred_element_type=F32)
            hid = (gate * jax.nn.sigmoid(gate) * up).astype(BF16)
            acc = acc + jnp.dot(hid, wd_ref[c * tf:(c + 1) * tf, :], preferred_element_type=F32)
        o_ref[sl, :] = _layer_norm_rows(DEEPNORM_ALPHA * x1 + acc, g2_ref[...], b2_ref[...])


def _post_mixer(a, w_out, x, g1, b1, w_gu, w_down, g2, b2, tm=1024, tf=256, n_sub=2):
    t, d = x.shape
    k = a.shape[1]
    d_ff = w_down.shape[0]
    tm = min(tm, t)
    tok = lambda i: (i, 0)
    const = lambda i: (0, 0)
    vec = pl.BlockSpec((1, d), const)
    resident = dict(pipeline_mode=pl.Buffered(1))
    return pl.pallas_call(
        functools.partial(_post_mixer_kernel, d_ff=d_ff, tf=tf, n_sub=n_sub),
        grid=(t // tm,),
        in_specs=[pl.BlockSpec((tm, k), tok),
                  pl.BlockSpec((k, d), const, **resident),
                  pl.BlockSpec((tm, d), tok),
                  vec, vec,
                  pl.BlockSpec((d, 2 * d_ff), const, **resident),
                  pl.BlockSpec((d_ff, d), const, **resident),
                  vec, vec],
        out_specs=pl.BlockSpec((tm, d), tok),
        out_shape=jax.ShapeDtypeStruct((t, d), F32),
        compiler_params=_params("parallel"),
        name="post_mixer",
    )(a, w_out, x, g1.reshape(1, d), b1.reshape(1, d), w_gu, w_down, g2.reshape(1, d), b2.reshape(1, d))


def _conv_kernel(x_ref, wb_ref, wc_ref, wh_ref, cw_ref, o_ref, xb_ref, *, n_sub):
    @pl.when(pl.program_id(1) == 0)
    def _():
        xb_ref[...] = x_ref[0].astype(BF16)

    xb = xb_ref[...]
    s = xb.shape[0]
    tc = wb_ref.shape[1] // n_sub
    row = lax.broadcasted_iota(jnp.int32, (s, 1), 0)
    cw = cw_ref[...]

    def gates(i):
        cols = slice(i * tc, (i + 1) * tc)
        return [jnp.dot(xb, w[:, cols], preferred_element_type=F32) for w in (wb_ref, wc_ref, wh_ref)]

    nxt_g = gates(0)
    for i in range(n_sub):
        bg, cg, hh = nxt_g
        if i + 1 < n_sub:
            nxt_g = gates(i + 1)
        cols = slice(i * tc, (i + 1) * tc)
        u = cg * hh
        prev = jnp.where(row == 0, 0.0, pltpu.roll(u, 1, axis=0))
        nxt = jnp.where(row == s - 1, 0.0, pltpu.roll(u, s - 1, axis=0))
        y = cw[0:1, cols] * prev + cw[1:2, cols] * u + cw[2:3, cols] * nxt
        o_ref[0, :, cols] = (bg * y).astype(o_ref.dtype)


def _conv_mixer(x3, w_in, conv_w, tc=512, n_sub=2):
    b, s, d = x3.shape
    nj = d // tc
    return pl.pallas_call(
        functools.partial(_conv_kernel, n_sub=n_sub),
        grid=(b, nj),
        in_specs=[pl.BlockSpec((1, s, d), lambda i, j: (i, 0, 0)),
                  pl.BlockSpec((d, tc), lambda i, j: (0, j)),
                  pl.BlockSpec((d, tc), lambda i, j: (0, nj + j)),
                  pl.BlockSpec((d, tc), lambda i, j: (0, 2 * nj + j)),
                  pl.BlockSpec((3, tc), lambda i, j: (0, j))],
        out_specs=pl.BlockSpec((1, s, tc), lambda i, j: (i, 0, j)),
        out_shape=jax.ShapeDtypeStruct((b, s, d), BF16),
        scratch_shapes=[pltpu.VMEM((s, d), BF16)],
        compiler_params=_params("parallel", "arbitrary"),
        name="conv_mixer",
    )(x3, w_in, w_in, w_in, conv_w)


def _diff_attn_kernel(q_ref, k_ref, v_ref, lam_ref, slope_ref, dist_ref, g_ref, o_ref, bias_ref,
                      *, lam_init, n_chain):
    tq = q_ref.shape[1]
    s = k_ref.shape[1]
    hd = LANES // 2
    rows = tq // n_chain
    nkb = s // rows

    @pl.when((pl.program_id(1) == 0) & (pl.program_id(2) == 0))
    def _():
        bias_ref[...] = dist_ref[...] * (slope_ref[0][:, 0:1] * LOG2E)

    lane = lax.broadcasted_iota(jnp.int32, (1, LANES), 1)
    m1 = jnp.where(lane < hd, 1.0, 0.0).astype(BF16)
    m2 = jnp.where(lane >= hd, 1.0, 0.0).astype(BF16)
    lam = lam_ref[...]
    lam_full = (jnp.exp(jnp.sum(lam[0:1] * lam[1:2], axis=-1, keepdims=True))
                - jnp.exp(jnp.sum(lam[2:3] * lam[3:4], axis=-1, keepdims=True)) + lam_init)
    k = k_ref[0]

    def scores(c):
        q = q_ref[0, c * rows:(c + 1) * rows, :]
        return _nt_dot(q * m1, k), _nt_dot(q * m2, k)

    v = v_ref[0]
    sc_next = scores(0)
    for c in range(n_chain):
        sc = sc_next
        if c + 1 < n_chain:
            sc_next = scores(c + 1)
        first = nkb - 1 - (pl.program_id(2) * n_chain + c)
        bias = jnp.concatenate([bias_ref[first + kb] for kb in range(nkb)], axis=1)
        s1 = sc[0] - bias
        s2 = sc[1] - bias
        e1 = jnp.exp2(s1 - jnp.max(s1, axis=-1, keepdims=True))
        e2 = jnp.exp2(s2 - jnp.max(s2, axis=-1, keepdims=True))
        l1 = jnp.sum(e1, axis=-1, keepdims=True)
        l2 = jnp.sum(e2, axis=-1, keepdims=True)
        a = (e1 - e2 * (lam_full * l1 / l2)).astype(BF16)
        o = jnp.dot(a, v, preferred_element_type=F32) * (1.0 / l1)
        ms = jnp.mean(o * o, axis=-1, keepdims=True)
        o = o * lax.rsqrt(ms + RMS_EPS) * g_ref[...] * (1.0 - lam_init)
        o_ref[0, c * rows:(c + 1) * rows, :] = o.astype(o_ref.dtype)


def _alibi_distance_blocks(s, rows):
    nblk = 2 * s // rows - 1
    r = np.arange(rows)[None, :, None]
    col = (np.arange(nblk)[:, None, None] * rows + np.arange(rows)[None, None, :])
    return jnp.asarray(np.abs(r - col + s - rows).astype(np.float32))


def _diff_attention(qkv, lam, subln_g, layer_idx, tq=1024, n_chain=4):
    b, s, d3 = qkv.shape
    d = d3 // 3
    nh = d // LANES
    tq = min(tq, s)
    rows = tq // n_chain
    dist = _alibi_distance_blocks(s, rows)
    lam_init = 0.8 - 0.6 * math.exp(-0.3 * layer_idx)
    slopes = np.array([2.0 ** (-8.0 * (h + 1) / nh) for h in range(nh)], dtype=np.float32)
    slopes = jnp.asarray(np.broadcast_to(slopes[:, None, None], (nh, 1, LANES)).copy())
    return pl.pallas_call(
        functools.partial(_diff_attn_kernel, lam_init=lam_init, n_chain=n_chain),
        grid=(nh, b, s // tq),
        in_specs=[pl.BlockSpec((1, tq, LANES), lambda h, i, j: (i, j, h)),
                  pl.BlockSpec((1, s, LANES), lambda h, i, j: (i, 0, nh + h)),
                  pl.BlockSpec((1, s, LANES), lambda h, i, j: (i, 0, 2 * nh + h)),
                  pl.BlockSpec(lam.shape, lambda h, i, j: (0, 0)),
                  pl.BlockSpec((1, 1, LANES), lambda h, i, j: (h, 0, 0)),
                  pl.BlockSpec(dist.shape, lambda h, i, j: (0, 0, 0)),
                  pl.BlockSpec((1, LANES), lambda h, i, j: (0, 0))],
        out_specs=pl.BlockSpec((1, tq, LANES), lambda h, i, j: (i, j, h)),
        out_shape=jax.ShapeDtypeStruct((b, s, d), BF16),
        scratch_shapes=[pltpu.VMEM(dist.shape, F32)],
        compiler_params=_params("arbitrary", "arbitrary", "arbitrary"),
        name="diff_attn",
    )(qkv, qkv, qkv, lam, slopes, dist, subln_g.reshape(1, LANES))


NA_BAND = 4


def _na_kernel(q_ref, k_ref, v_ref, tbl_ref, o_ref, *, rows, kr):
    w = GRID_W
    hd = LANES // 2
    nband = rows // NA_BAND
    kwin = NA_BAND + kr
    lane = lax.broadcasted_iota(jnp.int32, (1, LANES), 1)
    m0 = jnp.where(lane < hd, 1.0, 0.0).astype(BF16)
    m1 = jnp.where(lane >= hd, 1.0, 0.0).astype(BF16)

    def window_start(band):
        return min(max(band * NA_BAND - kr // 2, 0), rows - kwin)

    def scores(band):
        q = q_ref[0, band * NA_BAND * w:(band + 1) * NA_BAND * w, :]
        qq = jnp.concatenate([q * m0, q * m1], axis=0)
        ws = window_start(band)
        return _nt_dot(qq, k_ref[0, ws * w:(ws + kwin) * w, :])

    nq = NA_BAND * w
    sc_next = scores(0)
    for band in range(nband):
        sc = sc_next
        if band + 1 < nband:
            sc_next = scores(band + 1)
        kind = 0 if band == 0 else (2 if band == nband - 1 else 1)
        sc = sc + tbl_ref[0, kind]
        e = jnp.exp2(sc - jnp.max(sc, axis=-1, keepdims=True))
        rl = 1.0 / jnp.sum(e, axis=-1, keepdims=True)
        ws = window_start(band)
        o = jnp.dot(e.astype(BF16), v_ref[0, ws * w:(ws + kwin) * w, :], preferred_element_type=F32) * rl
        out = jnp.where(lane < hd, o[:nq], o[nq:])
        o_ref[0, band * nq:(band + 1) * nq, :] = out.astype(o_ref.dtype)


def _na_bias_table(rpb, rows, kr):
    kc = NA_WIN_COLS
    w = GRID_W
    kwin = NA_BAND + kr
    nband = rows // NA_BAND
    col = np.arange(w)
    cs = np.clip(col - kc // 2, 0, w - kc)
    col_ok = (col[None, :] >= cs[:, None]) & (col[None, :] < cs[:, None] + kc)
    dc = np.clip(col[None, :] - col[:, None], -(kc - 1), kc - 1) + (kc - 1)
    bias = jnp.where(jnp.asarray(col_ok), rpb[:, :, dc] * LOG2E, -jnp.inf)

    def build(bias_h):
        masked = jnp.full(bias_h[:, 0].shape, -jnp.inf, F32)
        kinds = []
        for band in (0, 1, nband - 1):
            ws = min(max(band * NA_BAND - kr // 2, 0), rows - kwin)
            row_blocks = []
            for a in range(NA_BAND):
                r = band * NA_BAND + a
                rs = min(max(r - kr // 2, 0), rows - kr)
                cols = [bias_h[:, ws + kk - r + (NA_MAX_ROWS - 1)] if rs <= ws + kk < rs + kr else masked
                        for kk in range(kwin)]
                row_blocks.append(jnp.concatenate(cols, axis=-1))
            kinds.append(jnp.concatenate(row_blocks, axis=-2))
        return jnp.stack(kinds, axis=1)

    return jnp.concatenate([build(bias[0::2]), build(bias[1::2])], axis=2)


def _neighborhood_attention(qkv, rpb):
    b, s, d3 = qkv.shape
    d = d3 // 3
    npair = d // LANES
    rows = s // GRID_W
    kr = min(NA_MAX_ROWS, rows)
    assert rows % NA_BAND == 0 and rows >= NA_BAND + kr
    tbl = _na_bias_table(rpb.astype(F32), rows, kr)
    return pl.pallas_call(
        functools.partial(_na_kernel, rows=rows, kr=kr),
        grid=(npair, b),
        in_specs=[pl.BlockSpec((1, s, LANES), lambda p, i: (i, 0, p)),
                  pl.BlockSpec((1, s, LANES), lambda p, i: (i, 0, npair + p)),
                  pl.BlockSpec((1, s, LANES), lambda p, i: (i, 0, 2 * npair + p)),
                  pl.BlockSpec((1,) + tbl.shape[1:], lambda p, i: (p, 0, 0, 0))],
        out_specs=pl.BlockSpec((1, s, LANES), lambda p, i: (i, 0, p)),
        out_shape=jax.ShapeDtypeStruct((b, s, d), BF16),
        compiler_params=_params("parallel", "parallel"),
        name="na_attn",
    )(qkv, qkv, qkv, tbl)


def _mla_proj_kernel(x_ref, wa_ref, gq_ref, gkv_ref, wuq_ref, wukv_ref, cos_ref, sin_ref,
                     qn_ref, qr_ref, kn_ref, kr_ref, v_ref, *, scale):
    qrank, kvrank = MLA_Q_RANK, MLA_KV_RANK
    a = jnp.dot(x_ref[...].astype(BF16), wa_ref[...], preferred_element_type=F32)
    cq = a[:, :qrank]
    cq = cq * lax.rsqrt(jnp.mean(cq * cq, axis=-1, keepdims=True) + RMS_EPS) * gq_ref[...]
    ckv = a[:, qrank:qrank + kvrank]
    ckv = ckv * lax.rsqrt(jnp.mean(ckv * ckv, axis=-1, keepdims=True) + RMS_EPS) * gkv_ref[...]
    cos1 = cos_ref[...]
    sin1 = sin_ref[...]
    o = qrank + kvrank
    kr_ref[...] = (a[:, o:o + LANES] * cos1 + a[:, o + LANES:o + 2 * LANES] * sin1).astype(kr_ref.dtype)
    reps = qr_ref.shape[1] // LANES
    cos = jnp.concatenate([cos1] * reps, axis=1)
    sin = jnp.concatenate([sin1] * reps, axis=1)
    q = jnp.dot(cq.astype(BF16), wuq_ref[...], preferred_element_type=F32)
    nn = qn_ref.shape[1]
    nr = qr_ref.shape[1]
    qn_ref[...] = (q[:, :nn] * scale).astype(qn_ref.dtype)
    qr_ref[...] = ((q[:, nn:nn + nr] * cos + q[:, nn + nr:nn + 2 * nr] * sin) * scale).astype(qr_ref.dtype)
    kv = jnp.dot(ckv.astype(BF16), wukv_ref[...], preferred_element_type=F32)
    kn_ref[...] = kv[:, :nn].astype(kn_ref.dtype)
    v_ref[...] = kv[:, nn:].astype(v_ref.dtype)


def _swap_halves(t):
    half = t.shape[-1] // 2
    return jnp.concatenate([t[..., half:], t[..., :half]], axis=-1)


def _mla_projection(x, w_a, g_q, g_kv, w_uq, w_ukv, seq, tm=512):
    t, d = x.shape
    nh, nope, rope, vd = MLA_HEADS, MLA_NOPE, MLA_ROPE, MLA_V
    qrank, kvrank = MLA_Q_RANK, MLA_KV_RANK
    tm = min(tm, seq)
    rep = LANES // rope
    w_kr = w_a[:, qrank + kvrank:]
    wa_ext = jnp.concatenate([w_a[:, :qrank + kvrank], jnp.tile(w_kr, (1, rep)),
                              jnp.tile(_swap_halves(w_kr), (1, rep))], axis=1).astype(BF16)
    wq = w_uq.reshape(qrank, nh, nope + rope)
    wq_rope = wq[:, :, nope:]
    wuq_ext = jnp.concatenate([wq[:, :, :nope].reshape(qrank, nh * nope),
                               wq_rope.reshape(qrank, nh * rope),
                               _swap_halves(wq_rope).reshape(qrank, nh * rope)], axis=1).astype(BF16)
    wkv = w_ukv.reshape(kvrank, nh, nope + vd)
    wukv_p = jnp.concatenate([wkv[:, :, :nope].reshape(kvrank, nh * nope),
                              wkv[:, :, nope:].reshape(kvrank, nh * vd)], axis=1).astype(BF16)
    inv_freq = 1.0 / (ROPE_THETA ** (jnp.arange(0, rope, 2, dtype=F32) / rope))
    ang = jnp.arange(seq, dtype=F32)[:, None] * inv_freq[None, :]
    cos, sin = jnp.cos(ang), jnp.sin(ang)
    cos_t = jnp.tile(jnp.concatenate([cos, cos], axis=-1), (1, rep))
    sin_t = jnp.tile(jnp.concatenate([-sin, sin], axis=-1), (1, rep))
    nblk = seq // tm
    scale = (nope + rope) ** -0.5 * LOG2E
    na = wa_ext.shape[1]
    nq = wuq_ext.shape[1]
    nkv = wukv_p.shape[1]
    const = lambda i: (0, 0)
    tok = lambda i: (i, 0)
    return pl.pallas_call(
        functools.partial(_mla_proj_kernel, scale=scale),
        grid=(t // tm,),
        in_specs=[pl.BlockSpec((tm, d), tok),
                  pl.BlockSpec((d, na), const),
                  pl.BlockSpec((1, qrank), const),
                  pl.BlockSpec((1, kvrank), const),
                  pl.BlockSpec((qrank, nq), const),
                  pl.BlockSpec((kvrank, nkv), const),
                  pl.BlockSpec((tm, LANES), lambda i: (i % nblk, 0)),
                  pl.BlockSpec((tm, LANES), lambda i: (i % nblk, 0))],
        out_specs=[pl.BlockSpec((tm, nh * nope), tok),
                   pl.BlockSpec((tm, nh * rope), tok),
                   pl.BlockSpec((tm, nh * nope), tok),
                   pl.BlockSpec((tm, LANES), tok),
                   pl.BlockSpec((tm, nh * vd), tok)],
        out_shape=[jax.ShapeDtypeStruct((t, nh * nope), BF16),
                   jax.ShapeDtypeStruct((t, nh * rope), BF16),
                   jax.ShapeDtypeStruct((t, nh * nope), BF16),
                   jax.ShapeDtypeStruct((t, LANES), BF16),
                   jax.ShapeDtypeStruct((t, nh * vd), BF16)],
        compiler_params=_params("parallel"),
        name="mla_proj",
    )(x, wa_ext, g_q.reshape(1, qrank), g_kv.reshape(1, kvrank), wuq_ext, wukv_p, cos_t, sin_t)


def _mla_attn_kernel(qn_ref, qr_ref, kn_ref, kr_ref, v_ref, o_ref, kcat_ref, *, n_sub):
    npair = kcat_ref.shape[0]
    nhead = 2 * npair
    rows = qn_ref.shape[1] // n_sub

    @pl.when(pl.program_id(2) == 0)
    def _():
        for pp in range(npair):
            kcat_ref[pp, :, :LANES] = kn_ref[0, :, pp * LANES:(pp + 1) * LANES]
            kcat_ref[pp, :, LANES:] = kr_ref[0]

    lane = lax.broadcasted_iota(jnp.int32, (1, LANES), 1)

    def scores(chain):
        sub, head = chain // nhead, chain % nhead
        pp, e = head // 2, head % 2
        qn = qn_ref[0, sub * rows:(sub + 1) * rows, pp * LANES:(pp + 1) * LANES]
        qr = qr_ref[0, sub * rows:(sub + 1) * rows, :]
        mn = jnp.where((lane >= e * MLA_NOPE) & (lane < (e + 1) * MLA_NOPE), 1.0, 0.0).astype(BF16)
        mr = jnp.where((lane >= head * MLA_ROPE) & (lane < (head + 1) * MLA_ROPE), 1.0, 0.0).astype(BF16)
        lhs = jnp.concatenate([qn * mn, qr * mr], axis=1)
        return _nt_dot(lhs, kcat_ref[pp])

    nchain = n_sub * nhead
    sc_next = scores(0)
    prev = None
    for chain in range(nchain):
        sc = sc_next
        if chain + 1 < nchain:
            sc_next = scores(chain + 1)
        sub, head = chain // nhead, chain % nhead
        pp = head // 2
        pe = jnp.exp2(sc - jnp.max(sc, axis=-1, keepdims=True))
        rl = 1.0 / jnp.sum(pe, axis=-1, keepdims=True)
        v = v_ref[0, :, pp * LANES:(pp + 1) * LANES]
        out = jnp.dot(pe.astype(BF16), v, preferred_element_type=F32) * rl
        if head % 2 == 1:
            o_ref[0, sub * rows:(sub + 1) * rows, pp * LANES:(pp + 1) * LANES] = jnp.where(
                lane < MLA_V, prev, out).astype(o_ref.dtype)
        prev = out


def _mla_attention(qn, qr, kn, kr, v, tq=1024, n_sub=2, heads_per_step=4):
    b, s, _ = qn.shape
    ngrp = MLA_HEADS // heads_per_step
    npair = heads_per_step // 2
    wn = npair * LANES
    assert heads_per_step * MLA_ROPE == LANES
    tq = min(tq, s)
    return pl.pallas_call(
        functools.partial(_mla_attn_kernel, n_sub=n_sub),
        grid=(b, ngrp, s // tq),
        in_specs=[pl.BlockSpec((1, tq, wn), lambda i, g, j: (i, j, g)),
                  pl.BlockSpec((1, tq, LANES), lambda i, g, j: (i, j, g)),
                  pl.BlockSpec((1, s, wn), lambda i, g, j: (i, 0, g)),
                  pl.BlockSpec((1, s, LANES), lambda i, g, j: (i, 0, 0)),
                  pl.BlockSpec((1, s, wn), lambda i, g, j: (i, 0, g))],
        out_specs=pl.BlockSpec((1, tq, wn), lambda i, g, j: (i, j, g)),
        out_shape=jax.ShapeDtypeStruct((b, s, ngrp * wn), BF16),
        scratch_shapes=[pltpu.VMEM((npair, s, 2 * LANES), BF16)],
        compiler_params=_params("parallel", "parallel", "arbitrary"),
        name="mla_attn",
    )(qn, qr, kn, kr, v)


def _scaled_qkv_weight(w_qkv, head_dim):
    d = w_qkv.shape[0]
    c = head_dim ** -0.5 * LOG2E
    return jnp.concatenate([w_qkv[:, :d] * c, w_qkv[:, d:]], axis=1).astype(BF16)


def kernel(x, conv_w_in, conv_w, conv_w_out, diff_w_qkv, diff_lambda, diff_subln_g, diff_w_out,
           na_w_qkv, na_rpb, na_w_out, mla_w_a, mla_g_q, mla_g_kv, mla_w_uq, mla_w_ukv, mla_w_out,
           ln1_g, ln1_b, ffn_w_gu, ffn_w_down, ln2_g, ln2_b):
    b, s, d = x.shape
    t = b * s
    xf = x.reshape(t, d)
    for i in range(DEPTH):
        m, j = i % N_MIXERS, i // N_MIXERS
        if m == 0:
            a = _conv_mixer(xf.reshape(b, s, d), conv_w_in[j].astype(BF16), conv_w[j])
            w_out = conv_w_out[j]
        elif m == 1:
            qkv = _proj(xf, _scaled_qkv_weight(diff_w_qkv[j], d // (2 * DA_HEADS)), BF16)
            a = _diff_attention(qkv.reshape(b, s, 3 * d), diff_lambda[j], diff_subln_g[j], i)
            w_out = diff_w_out[j]
        elif m == 2:
            qkv = _proj(xf, _scaled_qkv_weight(na_w_qkv[j], d // NA_HEADS), BF16)
            a = _neighborhood_attention(qkv.reshape(b, s, 3 * d), na_rpb[j])
            w_out = na_w_out[j]
        else:
            qn, qr, kn, kr, v = _mla_projection(xf, mla_w_a[j], mla_g_q[j], mla_g_kv[j],
                                                mla_w_uq[j], mla_w_ukv[j], s)
            r3 = lambda z: z.reshape(b, s, z.shape[-1])
            a = _mla_attention(r3(qn), r3(qr), r3(kn), r3(kr), r3(v))
            w_out = mla_w_out[j]
        xf = _post_mixer(a.reshape(t, -1), w_out.astype(BF16), xf, ln1_g[i], ln1_b[i],
                         ffn_w_gu[i].astype(BF16), ffn_w_down[i].astype(BF16), ln2_g[i], ln2_b[i])
    return xf.reshape(b, s, d)
```

```python
import functools
import math

import numpy as np
import jax
import jax.numpy as jnp
from jax import lax
from jax.experimental import pallas as pl
from jax.experimental.pallas import tpu as pltpu

DEPTH = 4
N_MIXERS = 4
GRID_W = 64
LN_EPS = 1e-5
RMS_EPS = 1e-6
LOG2E = math.log2(math.e)
DEEPNORM_ALPHA = (2 * DEPTH) ** 0.25
DA_HEADS = 8
NA_HEADS = 16
NA_MAX_ROWS = 8
NA_WIN_COLS = 16
MLA_HEADS = 16
MLA_Q_RANK = 256
MLA_KV_RANK = 128
MLA_NOPE = 64
MLA_ROPE = 32
MLA_V = 64
ROPE_THETA = 10000.0

LANES = 128
VMEM_LIMIT = 56 * 1024 * 1024
VMEM_LIMIT_MLA = 60 * 1024 * 1024
BF16 = jnp.bfloat16
F32 = jnp.float32


def _params(*sem, vmem=VMEM_LIMIT):
    return pltpu.CompilerParams(dimension_semantics=sem, vmem_limit_bytes=vmem)


def _layer_norm_rows(y, g, b):
    mu = jnp.mean(y, axis=-1, keepdims=True)
    yc = y - mu
    var = jnp.mean(yc * yc, axis=-1, keepdims=True)
    return yc * lax.rsqrt(var + LN_EPS) * g + b


def _nt_dot(a, b):
    return lax.dot_general(a, b, (((1,), (1,)), ((), ())), preferred_element_type=F32)


def _proj_kernel(x_ref, w_ref, o_ref):
    o_ref[...] = jnp.dot(x_ref[...].astype(BF16), w_ref[...], preferred_element_type=F32).astype(o_ref.dtype)


def _proj(x, w, out_dtype, tm=1024):
    t, k = x.shape
    n = w.shape[1]
    tm = min(tm, t)
    return pl.pallas_call(
        _proj_kernel,
        grid=(t // tm,),
        in_specs=[pl.BlockSpec((tm, k), lambda i: (i, 0)),
                  pl.BlockSpec((k, n), lambda i: (0, 0), pipeline_mode=pl.Buffered(1))],
        out_specs=pl.BlockSpec((tm, n), lambda i: (i, 0)),
        out_shape=jax.ShapeDtypeStruct((t, n), out_dtype),
        compiler_params=_params("parallel"),
        name="proj",
    )(x, w)


def _post_mixer_kernel(a_ref, wo_ref, x_ref, g1_ref, b1_ref, wgu_ref, wd_ref, g2_ref, b2_ref, o_ref,
                       *, d_ff, tf, n_sub):
    rows = x_ref.shape[0] // n_sub
    hs = [jnp.dot(a_ref[i * rows:(i + 1) * rows, :], wo_ref[...], preferred_element_type=F32)
          for i in range(n_sub)]
    for i in range(n_sub):
        sl = slice(i * rows, (i + 1) * rows)
        x1 = _layer_norm_rows(DEEPNORM_ALPHA * x_ref[sl, :] + hs[i], g1_ref[...], b1_ref[...])
        xb = x1.astype(BF16)
        acc = jnp.zeros(x1.shape, F32)
        for c in range(d_ff // tf):
            gate = jnp.dot(xb, wgu_ref[:, c * tf:(c + 1) * tf], preferred_element_type=F32)
            up = jnp.dot(xb, wgu_ref[:, d_ff + c * tf:d_ff + (c + 1) * tf], preferred_element_type=F32)
            hid = (gate * jax.nn.sigmoid(gate) * up).astype(BF16)
            acc = acc + jnp.dot(hid, wd_ref[c * tf:(c + 1) * tf, :], preferred_element_type=F32)
        o_ref[sl, :] = _layer_norm_rows(DEEPNORM_ALPHA * x1 + acc, g2_ref[...], b2_ref[...])


def _post_mixer(a, w_out, x, g1, b1, w_gu, w_down, g2, b2, tm=1024, tf=256, n_sub=2):
    t, d = x.shape
    k = a.shape[1]
    d_ff = w_down.shape[0]
    tm = min(tm, t)
    tok = lambda i: (i, 0)
    const = lambda i: (0, 0)
    vec = pl.BlockSpec((1, d), const)
    resident = dict(pipeline_mode=pl.Buffered(1))
    return pl.pallas_call(
        functools.partial(_post_mixer_kernel, d_ff=d_ff, tf=tf, n_sub=n_sub),
        grid=(t // tm,),
        in_specs=[pl.BlockSpec((tm, k), tok),
                  pl.BlockSpec((k, d), const, **resident),
                  pl.BlockSpec((tm, d), tok),
                  vec, vec,
                  pl.BlockSpec((d, 2 * d_ff), const, **resident),
                  pl.BlockSpec((d_ff, d), const, **resident),
                  vec, vec],
        out_specs=pl.BlockSpec((tm, d), tok),
        out_shape=jax.ShapeDtypeStruct((t, d), F32),
        compiler_params=_params("parallel"),
        name="post_mixer",
    )(a, w_out, x, g1.reshape(1, d), b1.reshape(1, d), w_gu, w_down, g2.reshape(1, d), b2.reshape(1, d))


def _conv_kernel(x_ref, wb_ref, wc_ref, wh_ref, cw_ref, o_ref, xb_ref, *, n_sub):
    @pl.when(pl.program_id(1) == 0)
    def _():
        xb_ref[...] = x_ref[0].astype(BF16)

    xb = xb_ref[...]
    s = xb.shape[0]
    tc = wb_ref.shape[1] // n_sub
    row = lax.broadcasted_iota(jnp.int32, (s, 1), 0)
    cw = cw_ref[...]

    def gates(i):
        cols = slice(i * tc, (i + 1) * tc)
        return [jnp.dot(xb, w[:, cols], preferred_element_type=F32) for w in (wb_ref, wc_ref, wh_ref)]

    nxt_g = gates(0)
    for i in range(n_sub):
        bg, cg, hh = nxt_g
        if i + 1 < n_sub:
            nxt_g = gates(i + 1)
        cols = slice(i * tc, (i + 1) * tc)
        u = cg * hh
        prev = jnp.where(row == 0, 0.0, pltpu.roll(u, 1, axis=0))
        nxt = jnp.where(row == s - 1, 0.0, pltpu.roll(u, s - 1, axis=0))
        y = cw[0:1, cols] * prev + cw[1:2, cols] * u + cw[2:3, cols] * nxt
        o_ref[0, :, cols] = (bg * y).astype(o_ref.dtype)


def _conv_mixer(x3, w_in, conv_w, tc=512, n_sub=2):
    b, s, d = x3.shape
    nj = d // tc
    return pl.pallas_call(
        functools.partial(_conv_kernel, n_sub=n_sub),
        grid=(b, nj),
        in_specs=[pl.BlockSpec((1, s, d), lambda i, j: (i, 0, 0)),
                  pl.BlockSpec((d, tc), lambda i, j: (0, j)),
                  pl.BlockSpec((d, tc), lambda i, j: (0, nj + j)),
                  pl.BlockSpec((d, tc), lambda i, j: (0, 2 * nj + j)),
                  pl.BlockSpec((3, tc), lambda i, j: (0, j))],
        out_specs=pl.BlockSpec((1, s, tc), lambda i, j: (i, 0, j)),
        out_shape=jax.ShapeDtypeStruct((b, s, d), BF16),
        scratch_shapes=[pltpu.VMEM((s, d), BF16)],
        compiler_params=_params("parallel", "arbitrary"),
        name="conv_mixer",
    )(x3, w_in, w_in, w_in, conv_w)


def _diff_attn_kernel(q_ref, k_ref, v_ref, lam_ref, slope_ref, dist_ref, g_ref, o_ref, bias_ref,
                      *, lam_init, n_chain):
    tq = q_ref.shape[1]
    s = k_ref.shape[1]
    hd = LANES // 2
    rows = tq // n_chain
    nkb = s // rows

    @pl.when((pl.program_id(1) == 0) & (pl.program_id(2) == 0))
    def _():
        bias_ref[...] = dist_ref[...] * (slope_ref[0][:, 0:1] * LOG2E)

    lane = lax.broadcasted_iota(jnp.int32, (1, LANES), 1)
    m1 = jnp.where(lane < hd, 1.0, 0.0).astype(BF16)
    m2 = jnp.where(lane >= hd, 1.0, 0.0).astype(BF16)
    lam = lam_ref[...]
    lam_full = (jnp.exp(jnp.sum(lam[0:1] * lam[1:2], axis=-1, keepdims=True))
                - jnp.exp(jnp.sum(lam[2:3] * lam[3:4], axis=-1, keepdims=True)) + lam_init)
    k = k_ref[0]

    def scores(c):
        q = q_ref[0, c * rows:(c + 1) * rows, :]
        return _nt_dot(q * m1, k), _nt_dot(q * m2, k)

    v = v_ref[0]
    sc_next = scores(0)
    for c in range(n_chain):
        sc = sc_next
        if c + 1 < n_chain:
            sc_next = scores(c + 1)
        first = nkb - 1 - (pl.program_id(2) * n_chain + c)
        bias = jnp.concatenate([bias_ref[first + kb] for kb in range(nkb)], axis=1)
        s1 = sc[0] - bias
        s2 = sc[1] - bias
        e1 = jnp.exp2(s1 - jnp.max(s1, axis=-1, keepdims=True))
        e2 = jnp.exp2(s2 - jnp.max(s2, axis=-1, keepdims=True))
        l1 = jnp.sum(e1, axis=-1, keepdims=True)
        l2 = jnp.sum(e2, axis=-1, keepdims=True)
        a = (e1 - e2 * (lam_full * l1 / l2)).astype(BF16)
        o = jnp.dot(a, v, preferred_element_type=F32) * (1.0 / l1)
        ms = jnp.mean(o * o, axis=-1, keepdims=True)
        o = o * lax.rsqrt(ms + RMS_EPS) * g_ref[...] * (1.0 - lam_init)
        o_ref[0, c * rows:(c + 1) * rows, :] = o.astype(o_ref.dtype)


def _alibi_distance_blocks(s, rows):
    nblk = 2 * s // rows - 1
    r = np.arange(rows)[None, :, None]
    col = (np.arange(nblk)[:, None, None] * rows + np.arange(rows)[None, None, :])
    return jnp.asarray(np.abs(r - col + s - rows).astype(np.float32))


def _diff_attention(qkv, lam, subln_g, layer_idx, tq=2048, n_chain=8):
    b, s, d3 = qkv.shape
    d = d3 // 3
    nh = d // LANES
    tq = min(tq, s)
    rows = tq // n_chain
    dist = _alibi_distance_blocks(s, rows)
    lam_init = 0.8 - 0.6 * math.exp(-0.3 * layer_idx)
    slopes = np.array([2.0 ** (-8.0 * (h + 1) / nh) for h in range(nh)], dtype=np.float32)
    slopes = jnp.asarray(np.broadcast_to(slopes[:, None, None], (nh, 1, LANES)).copy())
    return pl.pallas_call(
        functools.partial(_diff_attn_kernel, lam_init=lam_init, n_chain=n_chain),
        grid=(nh, b, s // tq),
        in_specs=[pl.BlockSpec((1, tq, LANES), lambda h, i, j: (i, j, h)),
                  pl.BlockSpec((1, s, LANES), lambda h, i, j: (i, 0, nh + h)),
                  pl.BlockSpec((1, s, LANES), lambda h, i, j: (i, 0, 2 * nh + h)),
                  pl.BlockSpec(lam.shape, lambda h, i, j: (0, 0)),
                  pl.BlockSpec((1, 1, LANES), lambda h, i, j: (h, 0, 0)),
                  pl.BlockSpec(dist.shape, lambda h, i, j: (0, 0, 0)),
                  pl.BlockSpec((1, LANES), lambda h, i, j: (0, 0))],
        out_specs=pl.BlockSpec((1, tq, LANES), lambda h, i, j: (i, j, h)),
        out_shape=jax.ShapeDtypeStruct((b, s, d), BF16),
        scratch_shapes=[pltpu.VMEM(dist.shape, F32)],
        compiler_params=_params("arbitrary", "arbitrary", "arbitrary"),
        name="diff_attn",
    )(qkv, qkv, qkv, lam, slopes, dist, subln_g.reshape(1, LANES))


NA_BAND = 4


def _na_kernel(q_ref, k_ref, v_ref, tbl_ref, o_ref, *, rows, kr):
    w = GRID_W
    hd = LANES // 2
    npair = q_ref.shape[2] // LANES
    nband = rows // NA_BAND
    kwin = NA_BAND + kr
    lane = lax.broadcasted_iota(jnp.int32, (1, LANES), 1)
    m0 = jnp.where(lane < hd, 1.0, 0.0).astype(BF16)
    m1 = jnp.where(lane >= hd, 1.0, 0.0).astype(BF16)

    def window_start(band):
        return min(max(band * NA_BAND - kr // 2, 0), rows - kwin)

    def scores(chain):
        pp, band = chain // nband, chain % nband
        cols = slice(pp * LANES, (pp + 1) * LANES)
        q = q_ref[0, band * NA_BAND * w:(band + 1) * NA_BAND * w, cols]
        qq = jnp.concatenate([q * m0, q * m1], axis=0)
        ws = window_start(band)
        return _nt_dot(qq, k_ref[0, ws * w:(ws + kwin) * w, cols])

    nq = NA_BAND * w
    nchain = npair * nband
    sc_next = scores(0)
    for chain in range(nchain):
        sc = sc_next
        if chain + 1 < nchain:
            sc_next = scores(chain + 1)
        pp, band = chain // nband, chain % nband
        cols = slice(pp * LANES, (pp + 1) * LANES)
        kind = 0 if band == 0 else (2 if band == nband - 1 else 1)
        sc = sc + tbl_ref[pp, kind]
        e = jnp.exp2(sc - jnp.max(sc, axis=-1, keepdims=True))
        rl = 1.0 / jnp.sum(e, axis=-1, keepdims=True)
        ws = window_start(band)
        o = jnp.dot(e.astype(BF16), v_ref[0, ws * w:(ws + kwin) * w, cols], preferred_element_type=F32) * rl
        out = jnp.where(lane < hd, o[:nq], o[nq:])
        o_ref[0, band * nq:(band + 1) * nq, cols] = out.astype(o_ref.dtype)


def _na_bias_table(rpb, rows, kr):
    kc = NA_WIN_COLS
    w = GRID_W
    kwin = NA_BAND + kr
    nband = rows // NA_BAND
    col = np.arange(w)
    cs = np.clip(col - kc // 2, 0, w - kc)
    col_ok = (col[None, :] >= cs[:, None]) & (col[None, :] < cs[:, None] + kc)
    dc = np.clip(col[None, :] - col[:, None], -(kc - 1), kc - 1) + (kc - 1)
    bias = jnp.where(jnp.asarray(col_ok), rpb[:, :, dc] * LOG2E, -jnp.inf)

    def build(bias_h):
        masked = jnp.full(bias_h[:, 0].shape, -jnp.inf, F32)
        kinds = []
        for band in (0, 1, nband - 1):
            ws = min(max(band * NA_BAND - kr // 2, 0), rows - kwin)
            row_blocks = []
            for a in range(NA_BAND):
                r = band * NA_BAND + a
                rs = min(max(r - kr // 2, 0), rows - kr)
                cols = [bias_h[:, ws + kk - r + (NA_MAX_ROWS - 1)] if rs <= ws + kk < rs + kr else masked
                        for kk in range(kwin)]
                row_blocks.append(jnp.concatenate(cols, axis=-1))
            kinds.append(jnp.concatenate(row_blocks, axis=-2))
        return jnp.stack(kinds, axis=1)

    return jnp.concatenate([build(bias[0::2]), build(bias[1::2])], axis=2)


def _neighborhood_attention(qkv, rpb, pairs_per_step=2):
    b, s, d3 = qkv.shape
    d = d3 // 3
    wl = pairs_per_step * LANES
    ngrp = d // wl
    rows = s // GRID_W
    kr = min(NA_MAX_ROWS, rows)
    assert rows % NA_BAND == 0 and rows >= NA_BAND + kr
    tbl = _na_bias_table(rpb.astype(F32), rows, kr)
    return pl.pallas_call(
        functools.partial(_na_kernel, rows=rows, kr=kr),
        grid=(ngrp, b),
        in_specs=[pl.BlockSpec((1, s, wl), lambda g, i: (i, 0, g)),
                  pl.BlockSpec((1, s, wl), lambda g, i: (i, 0, ngrp + g)),
                  pl.BlockSpec((1, s, wl), lambda g, i: (i, 0, 2 * ngrp + g)),
                  pl.BlockSpec((pairs_per_step,) + tbl.shape[1:], lambda g, i: (g, 0, 0, 0))],
        out_specs=pl.BlockSpec((1, s, wl), lambda g, i: (i, 0, g)),
        out_shape=jax.ShapeDtypeStruct((b, s, d), BF16),
        compiler_params=_params("parallel", "parallel"),
        name="na_attn",
    )(qkv, qkv, qkv, tbl)


def _mla_proj_kernel(x_ref, wa_ref, gq_ref, gkv_ref, wuq_ref, wukv_ref, cos_ref, sin_ref,
                     qn_ref, qr_ref, kn_ref, kr_ref, v_ref, *, scale):
    qrank, kvrank = MLA_Q_RANK, MLA_KV_RANK
    a = jnp.dot(x_ref[...].astype(BF16), wa_ref[...], preferred_element_type=F32)
    cq = a[:, :qrank]
    cq = cq * lax.rsqrt(jnp.mean(cq * cq, axis=-1, keepdims=True) + RMS_EPS) * gq_ref[...]
    ckv = a[:, qrank:qrank + kvrank]
    ckv = ckv * lax.rsqrt(jnp.mean(ckv * ckv, axis=-1, keepdims=True) + RMS_EPS) * gkv_ref[...]
    cos1 = cos_ref[...]
    sin1 = sin_ref[...]
    o = qrank + kvrank
    kr_ref[...] = (a[:, o:o + LANES] * cos1 + a[:, o + LANES:o + 2 * LANES] * sin1).astype(kr_ref.dtype)
    reps = qr_ref.shape[1] // LANES
    cos = jnp.concatenate([cos1] * reps, axis=1)
    sin = jnp.concatenate([sin1] * reps, axis=1)
    q = jnp.dot(cq.astype(BF16), wuq_ref[...], preferred_element_type=F32)
    nn = qn_ref.shape[1]
    nr = qr_ref.shape[1]
    qn_ref[...] = (q[:, :nn] * scale).astype(qn_ref.dtype)
    qr_ref[...] = ((q[:, nn:nn + nr] * cos + q[:, nn + nr:nn + 2 * nr] * sin) * scale).astype(qr_ref.dtype)
    kv = jnp.dot(ckv.astype(BF16), wukv_ref[...], preferred_element_type=F32)
    kn_ref[...] = kv[:, :nn].astype(kn_ref.dtype)
    v_ref[...] = kv[:, nn:].astype(v_ref.dtype)


def _swap_halves(t):
    half = t.shape[-1] // 2
    return jnp.concatenate([t[..., half:], t[..., :half]], axis=-1)


def _mla_projection(x, w_a, g_q, g_kv, w_uq, w_ukv, seq, tm=512):
    t, d = x.shape
    nh, nope, rope, vd = MLA_HEADS, MLA_NOPE, MLA_ROPE, MLA_V
    qrank, kvrank = MLA_Q_RANK, MLA_KV_RANK
    tm = min(tm, seq)
    rep = LANES // rope
    w_kr = w_a[:, qrank + kvrank:]
    wa_ext = jnp.concatenate([w_a[:, :qrank + kvrank], jnp.tile(w_kr, (1, rep)),
                              jnp.tile(_swap_halves(w_kr), (1, rep))], axis=1).astype(BF16)
    wq = w_uq.reshape(qrank, nh, nope + rope)
    wq_rope = wq[:, :, nope:]
    wuq_ext = jnp.concatenate([wq[:, :, :nope].reshape(qrank, nh * nope),
                               wq_rope.reshape(qrank, nh * rope),
                               _swap_halves(wq_rope).reshape(qrank, nh * rope)], axis=1).astype(BF16)
    wkv = w_ukv.reshape(kvrank, nh, nope + vd)
    wukv_p = jnp.concatenate([wkv[:, :, :nope].reshape(kvrank, nh * nope),
                              wkv[:, :, nope:].reshape(kvrank, nh * vd)], axis=1).astype(BF16)
    inv_freq = 1.0 / (ROPE_THETA ** (jnp.arange(0, rope, 2, dtype=F32) / rope))
    ang = jnp.arange(seq, dtype=F32)[:, None] * inv_freq[None, :]
    cos, sin = jnp.cos(ang), jnp.sin(ang)
    cos_t = jnp.tile(jnp.concatenate([cos, cos], axis=-1), (1, rep))
    sin_t = jnp.tile(jnp.concatenate([-sin, sin], axis=-1), (1, rep))
    nblk = seq // tm
    scale = (nope + rope) ** -0.5 * LOG2E
    na = wa_ext.shape[1]
    nq = wuq_ext.shape[1]
    nkv = wukv_p.shape[1]
    const = lambda i: (0, 0)
    tok = lambda i: (i, 0)
    return pl.pallas_call(
        functools.partial(_mla_proj_kernel, scale=scale),
        grid=(t // tm,),
        in_specs=[pl.BlockSpec((tm, d), tok),
                  pl.BlockSpec((d, na), const),
                  pl.BlockSpec((1, qrank), const),
                  pl.BlockSpec((1, kvrank), const),
                  pl.BlockSpec((qrank, nq), const),
                  pl.BlockSpec((kvrank, nkv), const),
                  pl.BlockSpec((tm, LANES), lambda i: (i % nblk, 0)),
                  pl.BlockSpec((tm, LANES), lambda i: (i % nblk, 0))],
        out_specs=[pl.BlockSpec((tm, nh * nope), tok),
                   pl.BlockSpec((tm, nh * rope), tok),
                   pl.BlockSpec((tm, nh * nope), tok),
                   pl.BlockSpec((tm, LANES), tok),
                   pl.BlockSpec((tm, nh * vd), tok)],
        out_shape=[jax.ShapeDtypeStruct((t, nh * nope), BF16),
                   jax.ShapeDtypeStruct((t, nh * rope), BF16),
                   jax.ShapeDtypeStruct((t, nh * nope), BF16),
                   jax.ShapeDtypeStruct((t, LANES), BF16),
                   jax.ShapeDtypeStruct((t, nh * vd), BF16)],
        compiler_params=_params("parallel"),
        name="mla_proj",
    )(x, wa_ext, g_q.reshape(1, qrank), g_kv.reshape(1, kvrank), wuq_ext, wukv_p, cos_t, sin_t)


def _mla_attn_kernel(qn_ref, qr_ref, kn_ref, kr_ref, v_ref, o_ref, kcat_ref, *, n_sub):
    npair = kcat_ref.shape[0]
    nhead = 2 * npair
    rows = qn_ref.shape[1] // n_sub

    @pl.when(pl.program_id(2) == 0)
    def _():
        for pp in range(npair):
            kcat_ref[pp, :, :LANES] = kn_ref[0, :, pp * LANES:(pp + 1) * LANES]
            kcat_ref[pp, :, LANES:] = kr_ref[0]

    lane = lax.broadcasted_iota(jnp.int32, (1, LANES), 1)

    def scores(chain):
        sub, head = chain // nhead, chain % nhead
        pp, e = head // 2, head % 2
        qn = qn_ref[0, sub * rows:(sub + 1) * rows, pp * LANES:(pp + 1) * LANES]
        qr = qr_ref[0, sub * rows:(sub + 1) * rows, :]
        mn = jnp.where((lane >= e * MLA_NOPE) & (lane < (e + 1) * MLA_NOPE), 1.0, 0.0).astype(BF16)
        mr = jnp.where((lane >= head * MLA_ROPE) & (lane < (head + 1) * MLA_ROPE), 1.0, 0.0).astype(BF16)
        lhs = jnp.concatenate([qn * mn, qr * mr], axis=1)
        return _nt_dot(lhs, kcat_ref[pp])

    nchain = n_sub * nhead
    sc_next = scores(0)
    prev = None
    for chain in range(nchain):
        sc = sc_next
        if chain + 1 < nchain:
            sc_next = scores(chain + 1)
        sub, head = chain // nhead, chain % nhead
        pp = head // 2
        pe = jnp.exp2(sc - jnp.max(sc, axis=-1, keepdims=True))
        rl = 1.0 / jnp.sum(pe, axis=-1, keepdims=True)
        v = v_ref[0, :, pp * LANES:(pp + 1) * LANES]
        out = jnp.dot(pe.astype(BF16), v, preferred_element_type=F32) * rl
        if head % 2 == 1:
            o_ref[0, sub * rows:(sub + 1) * rows, pp * LANES:(pp + 1) * LANES] = jnp.where(
                lane < MLA_V, prev, out).astype(o_ref.dtype)
        prev = out


def _mla_attention(qn, qr, kn, kr, v, tq=2048, n_sub=4, heads_per_step=4):
    b, s, _ = qn.shape
    ngrp = MLA_HEADS // heads_per_step
    npair = heads_per_step // 2
    wn = npair * LANES
    assert heads_per_step * MLA_ROPE == LANES
    tq = min(tq, s)
    return pl.pallas_call(
        functools.partial(_mla_attn_kernel, n_sub=n_sub),
        grid=(b, ngrp, s // tq),
        in_specs=[pl.BlockSpec((1, tq, wn), lambda i, g, j: (i, j, g)),
                  pl.BlockSpec((1, tq, LANES), lambda i, g, j: (i, j, g)),
                  pl.BlockSpec((1, s, wn), lambda i, g, j: (i, 0, g)),
                  pl.BlockSpec((1, s, LANES), lambda i, g, j: (i, 0, 0)),
                  pl.BlockSpec((1, s, wn), lambda i, g, j: (i, 0, g))],
        out_specs=pl.BlockSpec((1, tq, wn), lambda i, g, j: (i, j, g)),
        out_shape=jax.ShapeDtypeStruct((b, s, ngrp * wn), BF16),
        scratch_shapes=[pltpu.VMEM((npair, s, 2 * LANES), BF16)],
        compiler_params=_params("parallel", "parallel", "arbitrary", vmem=VMEM_LIMIT_MLA),
        name="mla_attn",
    )(qn, qr, kn, kr, v)


def _scaled_qkv_weight(w_qkv, head_dim):
    d = w_qkv.shape[0]
    c = head_dim ** -0.5 * LOG2E
    return jnp.concatenate([w_qkv[:, :d] * c, w_qkv[:, d:]], axis=1).astype(BF16)


def kernel(x, conv_w_in, conv_w, conv_w_out, diff_w_qkv, diff_lambda, diff_subln_g, diff_w_out,
           na_w_qkv, na_rpb, na_w_out, mla_w_a, mla_g_q, mla_g_kv, mla_w_uq, mla_w_ukv, mla_w_out,
           ln1_g, ln1_b, ffn_w_gu, ffn_w_down, ln2_g, ln2_b):
    b, s, d = x.shape
    t = b * s
    xf = x.reshape(t, d)
    for i in range(DEPTH):
        m, j = i % N_MIXERS, i // N_MIXERS
        if m == 0:
            a = _conv_mixer(xf.reshape(b, s, d), conv_w_in[j].astype(BF16), conv_w[j])
            w_out = conv_w_out[j]
        elif m == 1:
            qkv = _proj(xf, _scaled_qkv_weight(diff_w_qkv[j], d // (2 * DA_HEADS)), BF16)
            a = _diff_attention(qkv.reshape(b, s, 3 * d), diff_lambda[j], diff_subln_g[j], i)
            w_out = diff_w_out[j]
        elif m == 2:
            qkv = _proj(xf, _scaled_qkv_weight(na_w_qkv[j], d // NA_HEADS), BF16)
            a = _neighborhood_attention(qkv.reshape(b, s, 3 * d), na_rpb[j])
            w_out = na_w_out[j]
        else:
            qn, qr, kn, kr, v = _mla_projection(xf, mla_w_a[j], mla_g_q[j], mla_g_kv[j],
                                                mla_w_uq[j], mla_w_ukv[j], s)
            r3 = lambda z: z.reshape(b, s, z.shape[-1])
            a = _mla_attention(r3(qn), r3(qr), r3(kn), r3(kr), r3(v))
            w_out = mla_w_out[j]
        xf = _post_mixer(a.reshape(t, -1), w_out.astype(BF16), xf, ln1_g[i], ln1_b[i],
                         ffn_w_gu[i].astype(BF16), ffn_w_down[i].astype(BF16), ln2_g[i], ln2_b[i])
    return xf.reshape(b, s, d)
```

```python
import functools
import math

import numpy as np
import jax
import jax.numpy as jnp
from jax import lax
from jax.experimental import pallas as pl
from jax.experimental.pallas import tpu as pltpu

DEPTH = 4
N_MIXERS = 4
GRID_W = 64
LN_EPS = 1e-5
RMS_EPS = 1e-6
LOG2E = math.log2(math.e)
DEEPNORM_ALPHA = (2 * DEPTH) ** 0.25
DA_HEADS = 8
NA_HEADS = 16
NA_MAX_ROWS = 8
NA_WIN_COLS = 16
MLA_HEADS = 16
MLA_Q_RANK = 256
MLA_KV_RANK = 128
MLA_NOPE = 64
MLA_ROPE = 32
MLA_V = 64
ROPE_THETA = 10000.0

LANES = 128
VMEM_LIMIT = 56 * 1024 * 1024
VMEM_LIMIT_MLA = 60 * 1024 * 1024
BF16 = jnp.bfloat16
F32 = jnp.float32


def _params(*sem, vmem=VMEM_LIMIT):
    return pltpu.CompilerParams(dimension_semantics=sem, vmem_limit_bytes=vmem)


def _layer_norm_rows(y, g, b):
    mu = jnp.mean(y, axis=-1, keepdims=True)
    yc = y - mu
    var = jnp.mean(yc * yc, axis=-1, keepdims=True)
    return yc * lax.rsqrt(var + LN_EPS) * g + b


def _nt_dot(a, b):
    return lax.dot_general(a, b, (((1,), (1,)), ((), ())), preferred_element_type=F32)


def _proj_kernel(x_ref, w_ref, o_ref):
    o_ref[...] = jnp.dot(x_ref[...].astype(BF16), w_ref[...], preferred_element_type=F32).astype(o_ref.dtype)


def _proj(x, w, out_dtype, tm=1024):
    t, k = x.shape
    n = w.shape[1]
    tm = min(tm, t)
    return pl.pallas_call(
        _proj_kernel,
        grid=(t // tm,),
        in_specs=[pl.BlockSpec((tm, k), lambda i: (i, 0)),
                  pl.BlockSpec((k, n), lambda i: (0, 0), pipeline_mode=pl.Buffered(1))],
        out_specs=pl.BlockSpec((tm, n), lambda i: (i, 0)),
        out_shape=jax.ShapeDtypeStruct((t, n), out_dtype),
        compiler_params=_params("parallel"),
        name="proj",
    )(x, w)


def _post_mixer_kernel(a_ref, wo_ref, x_ref, g1_ref, b1_ref, wgu_ref, wd_ref, g2_ref, b2_ref, o_ref,
                       *, d_ff, tf, n_sub):
    rows = x_ref.shape[0] // n_sub
    hs = [jnp.dot(a_ref[i * rows:(i + 1) * rows, :], wo_ref[...], preferred_element_type=F32)
          for i in range(n_sub)]
    for i in range(n_sub):
        sl = slice(i * rows, (i + 1) * rows)
        x1 = _layer_norm_rows(DEEPNORM_ALPHA * x_ref[sl, :] + hs[i], g1_ref[...], b1_ref[...])
        xb = x1.astype(BF16)
        acc = jnp.zeros(x1.shape, F32)
        for c in range(d_ff // tf):
            gate = jnp.dot(xb, wgu_ref[:, c * tf:(c + 1) * tf], preferred_element_type=F32)
            up = jnp.dot(xb, wgu_ref[:, d_ff + c * tf:d_ff + (c + 1) * tf], preferred_element_type=F32)
            hid = (gate * jax.nn.sigmoid(gate) * up).astype(BF16)
            acc = acc + jnp.dot(hid, wd_ref[c * tf:(c + 1) * tf, :], preferred_element_type=F32)
        o_ref[sl, :] = _layer_norm_rows(DEEPNORM_ALPHA * x1 + acc, g2_ref[...], b2_ref[...])


def _post_mixer(a, w_out, x, g1, b1, w_gu, w_down, g2, b2, tm=1024, tf=256, n_sub=2):
    t, d = x.shape
    k = a.shape[1]
    d_ff = w_down.shape[0]
    tm = min(tm, t)
    tok = lambda i: (i, 0)
    const = lambda i: (0, 0)
    vec = pl.BlockSpec((1, d), const)
    resident = dict(pipeline_mode=pl.Buffered(1))
    return pl.pallas_call(
        functools.partial(_post_mixer_kernel, d_ff=d_ff, tf=tf, n_sub=n_sub),
        grid=(t // tm,),
        in_specs=[pl.BlockSpec((tm, k), tok),
                  pl.BlockSpec((k, d), const, **resident),
                  pl.BlockSpec((tm, d), tok),
                  vec, vec,
                  pl.BlockSpec((d, 2 * d_ff), const, **resident),
                  pl.BlockSpec((d_ff, d), const, **resident),
                  vec, vec],
        out_specs=pl.BlockSpec((tm, d), tok),
        out_shape=jax.ShapeDtypeStruct((t, d), F32),
        compiler_params=_params("parallel"),
        name="post_mixer",
    )(a, w_out, x, g1.reshape(1, d), b1.reshape(1, d), w_gu, w_down, g2.reshape(1, d), b2.reshape(1, d))


def _conv_kernel(x_ref, wb_ref, wc_ref, wh_ref, cw_ref, o_ref, xb_ref, *, n_sub):
    @pl.when(pl.program_id(1) == 0)
    def _():
        xb_ref[...] = x_ref[0].astype(BF16)

    xb = xb_ref[...]
    s = xb.shape[0]
    tc = wb_ref.shape[1] // n_sub
    row = lax.broadcasted_iota(jnp.int32, (s, 1), 0)
    cw = cw_ref[...]

    def gates(i):
        cols = slice(i * tc, (i + 1) * tc)
        return [jnp.dot(xb, w[:, cols], preferred_element_type=F32) for w in (wb_ref, wc_ref, wh_ref)]

    nxt_g = gates(0)
    for i in range(n_sub):
        bg, cg, hh = nxt_g
        if i + 1 < n_sub:
            nxt_g = gates(i + 1)
        cols = slice(i * tc, (i + 1) * tc)
        u = cg * hh
        prev = jnp.where(row == 0, 0.0, pltpu.roll(u, 1, axis=0))
        nxt = jnp.where(row == s - 1, 0.0, pltpu.roll(u, s - 1, axis=0))
        y = cw[0:1, cols] * prev + cw[1:2, cols] * u + cw[2:3, cols] * nxt
        o_ref[0, :, cols] = (bg * y).astype(o_ref.dtype)


def _conv_mixer(x3, w_in, conv_w, tc=512, n_sub=2):
    b, s, d = x3.shape
    nj = d // tc
    return pl.pallas_call(
        functools.partial(_conv_kernel, n_sub=n_sub),
        grid=(b, nj),
        in_specs=[pl.BlockSpec((1, s, d), lambda i, j: (i, 0, 0)),
                  pl.BlockSpec((d, tc), lambda i, j: (0, j)),
                  pl.BlockSpec((d, tc), lambda i, j: (0, nj + j)),
                  pl.BlockSpec((d, tc), lambda i, j: (0, 2 * nj + j)),
                  pl.BlockSpec((3, tc), lambda i, j: (0, j))],
        out_specs=pl.BlockSpec((1, s, tc), lambda i, j: (i, 0, j)),
        out_shape=jax.ShapeDtypeStruct((b, s, d), BF16),
        scratch_shapes=[pltpu.VMEM((s, d), BF16)],
        compiler_params=_params("parallel", "arbitrary"),
        name="conv_mixer",
    )(x3, w_in, w_in, w_in, conv_w)


def _diff_attn_kernel(q_ref, k_ref, v_ref, lam_ref, slope_ref, dist_ref, g_ref, o_ref, bias_ref,
                      *, lam_init, n_chain):
    tq = q_ref.shape[1]
    s = k_ref.shape[1]
    hd = LANES // 2
    rows = tq // n_chain
    nkb = s // rows

    @pl.when((pl.program_id(1) == 0) & (pl.program_id(2) == 0))
    def _():
        bias_ref[...] = dist_ref[...] * (slope_ref[0][:, 0:1] * LOG2E)

    lane = lax.broadcasted_iota(jnp.int32, (1, LANES), 1)
    m1 = jnp.where(lane < hd, 1.0, 0.0).astype(BF16)
    m2 = jnp.where(lane >= hd, 1.0, 0.0).astype(BF16)
    lam = lam_ref[...]
    lam_full = (jnp.exp(jnp.sum(lam[0:1] * lam[1:2], axis=-1, keepdims=True))
                - jnp.exp(jnp.sum(lam[2:3] * lam[3:4], axis=-1, keepdims=True)) + lam_init)
    k = k_ref[0]

    def scores(c):
        q = q_ref[0, c * rows:(c + 1) * rows, :]
        return _nt_dot(q * m1, k), _nt_dot(q * m2, k)

    v = v_ref[0]
    sc_next = scores(0)
    for c in range(n_chain):
        sc = sc_next
        if c + 1 < n_chain:
            sc_next = scores(c + 1)
        first = nkb - 1 - (pl.program_id(2) * n_chain + c)
        bias = jnp.concatenate([bias_ref[first + kb] for kb in range(nkb)], axis=1)
        s1 = sc[0] - bias
        s2 = sc[1] - bias
        e1 = jnp.exp2(s1 - jnp.max(s1, axis=-1, keepdims=True))
        e2 = jnp.exp2(s2 - jnp.max(s2, axis=-1, keepdims=True))
        l1 = jnp.sum(e1, axis=-1, keepdims=True)
        l2 = jnp.sum(e2, axis=-1, keepdims=True)
        a = (e1 - e2 * (lam_full * l1 / l2)).astype(BF16)
        o = jnp.dot(a, v, preferred_element_type=F32) * (1.0 / l1)
        ms = jnp.mean(o * o, axis=-1, keepdims=True)
        o = o * lax.rsqrt(ms + RMS_EPS) * g_ref[...] * (1.0 - lam_init)
        o_ref[0, c * rows:(c + 1) * rows, :] = o.astype(o_ref.dtype)


def _alibi_distance_blocks(s, rows):
    nblk = 2 * s // rows - 1
    r = np.arange(rows)[None, :, None]
    col = (np.arange(nblk)[:, None, None] * rows + np.arange(rows)[None, None, :])
    return jnp.asarray(np.abs(r - col + s - rows).astype(np.float32))


def _diff_attention(qkv, lam, subln_g, layer_idx, tq=2048, n_chain=16):
    b, s, d3 = qkv.shape
    d = d3 // 3
    nh = d // LANES
    tq = min(tq, s)
    rows = tq // n_chain
    dist = _alibi_distance_blocks(s, rows)
    lam_init = 0.8 - 0.6 * math.exp(-0.3 * layer_idx)
    slopes = np.array([2.0 ** (-8.0 * (h + 1) / nh) for h in range(nh)], dtype=np.float32)
    slopes = jnp.asarray(np.broadcast_to(slopes[:, None, None], (nh, 1, LANES)).copy())
    return pl.pallas_call(
        functools.partial(_diff_attn_kernel, lam_init=lam_init, n_chain=n_chain),
        grid=(nh, b, s // tq),
        in_specs=[pl.BlockSpec((1, tq, LANES), lambda h, i, j: (i, j, h)),
                  pl.BlockSpec((1, s, LANES), lambda h, i, j: (i, 0, nh + h)),
                  pl.BlockSpec((1, s, LANES), lambda h, i, j: (i, 0, 2 * nh + h)),
                  pl.BlockSpec(lam.shape, lambda h, i, j: (0, 0)),
                  pl.BlockSpec((1, 1, LANES), lambda h, i, j: (h, 0, 0)),
                  pl.BlockSpec(dist.shape, lambda h, i, j: (0, 0, 0)),
                  pl.BlockSpec((1, LANES), lambda h, i, j: (0, 0))],
        out_specs=pl.BlockSpec((1, tq, LANES), lambda h, i, j: (i, j, h)),
        out_shape=jax.ShapeDtypeStruct((b, s, d), BF16),
        scratch_shapes=[pltpu.VMEM(dist.shape, F32)],
        compiler_params=_params("arbitrary", "arbitrary", "arbitrary"),
        name="diff_attn",
    )(qkv, qkv, qkv, lam, slopes, dist, subln_g.reshape(1, LANES))


NA_BAND = 4


def _na_kernel(q_ref, k_ref, v_ref, tbl_ref, o_ref, *, rows, kr):
    w = GRID_W
    hd = LANES // 2
    npair = q_ref.shape[2] // LANES
    nband = rows // NA_BAND
    kwin = NA_BAND + kr
    lane = lax.broadcasted_iota(jnp.int32, (1, LANES), 1)
    m0 = jnp.where(lane < hd, 1.0, 0.0).astype(BF16)
    m1 = jnp.where(lane >= hd, 1.0, 0.0).astype(BF16)

    def window_start(band):
        return min(max(band * NA_BAND - kr // 2, 0), rows - kwin)

    def scores(chain):
        pp, band = chain // nband, chain % nband
        cols = slice(pp * LANES, (pp + 1) * LANES)
        q = q_ref[0, band * NA_BAND * w:(band + 1) * NA_BAND * w, cols]
        qq = jnp.concatenate([q * m0, q * m1], axis=0)
        ws = window_start(band)
        return _nt_dot(qq, k_ref[0, ws * w:(ws + kwin) * w, cols])

    nq = NA_BAND * w
    nchain = npair * nband
    sc_next = scores(0)
    for chain in range(nchain):
        sc = sc_next
        if chain + 1 < nchain:
            sc_next = scores(chain + 1)
        pp, band = chain // nband, chain % nband
        cols = slice(pp * LANES, (pp + 1) * LANES)
        kind = 0 if band == 0 else (2 if band == nband - 1 else 1)
        sc = sc + tbl_ref[pp, kind]
        e = jnp.exp2(sc - jnp.max(sc, axis=-1, keepdims=True))
        rl = 1.0 / jnp.sum(e, axis=-1, keepdims=True)
        ws = window_start(band)
        o = jnp.dot(e.astype(BF16), v_ref[0, ws * w:(ws + kwin) * w, cols], preferred_element_type=F32) * rl
        out = jnp.where(lane < hd, o[:nq], o[nq:])
        o_ref[0, band * nq:(band + 1) * nq, cols] = out.astype(o_ref.dtype)


def _na_bias_table(rpb, rows, kr):
    kc = NA_WIN_COLS
    w = GRID_W
    kwin = NA_BAND + kr
    nband = rows // NA_BAND
    col = np.arange(w)
    cs = np.clip(col - kc // 2, 0, w - kc)
    col_ok = (col[None, :] >= cs[:, None]) & (col[None, :] < cs[:, None] + kc)
    dc = np.clip(col[None, :] - col[:, None], -(kc - 1), kc - 1) + (kc - 1)
    bias = jnp.where(jnp.asarray(col_ok), rpb[:, :, dc] * LOG2E, -jnp.inf)

    def build(bias_h):
        masked = jnp.full(bias_h[:, 0].shape, -jnp.inf, F32)
        kinds = []
        for band in (0, 1, nband - 1):
            ws = min(max(band * NA_BAND - kr // 2, 0), rows - kwin)
            row_blocks = []
            for a in range(NA_BAND):
                r = band * NA_BAND + a
                rs = min(max(r - kr // 2, 0), rows - kr)
                cols = [bias_h[:, ws + kk - r + (NA_MAX_ROWS - 1)] if rs <= ws + kk < rs + kr else masked
                        for kk in range(kwin)]
                row_blocks.append(jnp.concatenate(cols, axis=-1))
            kinds.append(jnp.concatenate(row_blocks, axis=-2))
        return jnp.stack(kinds, axis=1)

    return jnp.concatenate([build(bias[0::2]), build(bias[1::2])], axis=2)


def _neighborhood_attention(qkv, rpb, pairs_per_step=2):
    b, s, d3 = qkv.shape
    d = d3 // 3
    wl = pairs_per_step * LANES
    ngrp = d // wl
    rows = s // GRID_W
    kr = min(NA_MAX_ROWS, rows)
    assert rows % NA_BAND == 0 and rows >= NA_BAND + kr
    tbl = _na_bias_table(rpb.astype(F32), rows, kr)
    return pl.pallas_call(
        functools.partial(_na_kernel, rows=rows, kr=kr),
        grid=(ngrp, b),
        in_specs=[pl.BlockSpec((1, s, wl), lambda g, i: (i, 0, g)),
                  pl.BlockSpec((1, s, wl), lambda g, i: (i, 0, ngrp + g)),
                  pl.BlockSpec((1, s, wl), lambda g, i: (i, 0, 2 * ngrp + g)),
                  pl.BlockSpec((pairs_per_step,) + tbl.shape[1:], lambda g, i: (g, 0, 0, 0))],
        out_specs=pl.BlockSpec((1, s, wl), lambda g, i: (i, 0, g)),
        out_shape=jax.ShapeDtypeStruct((b, s, d), BF16),
        compiler_params=_params("parallel", "parallel"),
        name="na_attn",
    )(qkv, qkv, qkv, tbl)


def _mla_proj_kernel(x_ref, wa_ref, gq_ref, gkv_ref, wuq_ref, wukv_ref, cos_ref, sin_ref,
                     qn_ref, qr_ref, kn_ref, kr_ref, v_ref, *, scale):
    qrank, kvrank = MLA_Q_RANK, MLA_KV_RANK
    a = jnp.dot(x_ref[...].astype(BF16), wa_ref[...], preferred_element_type=F32)
    cq = a[:, :qrank]
    cq = cq * lax.rsqrt(jnp.mean(cq * cq, axis=-1, keepdims=True) + RMS_EPS) * gq_ref[...]
    ckv = a[:, qrank:qrank + kvrank]
    ckv = ckv * lax.rsqrt(jnp.mean(ckv * ckv, axis=-1, keepdims=True) + RMS_EPS) * gkv_ref[...]
    cos1 = cos_ref[...]
    sin1 = sin_ref[...]
    o = qrank + kvrank
    kr_ref[...] = (a[:, o:o + LANES] * cos1 + a[:, o + LANES:o + 2 * LANES] * sin1).astype(kr_ref.dtype)
    reps = qr_ref.shape[1] // LANES
    cos = jnp.concatenate([cos1] * reps, axis=1)
    sin = jnp.concatenate([sin1] * reps, axis=1)
    q = jnp.dot(cq.astype(BF16), wuq_ref[...], preferred_element_type=F32)
    nn = qn_ref.shape[1]
    nr = qr_ref.shape[1]
    qn_ref[...] = (q[:, :nn] * scale).astype(qn_ref.dtype)
    qr_ref[...] = ((q[:, nn:nn + nr] * cos + q[:, nn + nr:nn + 2 * nr] * sin) * scale).astype(qr_ref.dtype)
    kv = jnp.dot(ckv.astype(BF16), wukv_ref[...], preferred_element_type=F32)
    kn_ref[...] = kv[:, :nn].astype(kn_ref.dtype)
    v_ref[...] = kv[:, nn:].astype(v_ref.dtype)


def _swap_halves(t):
    half = t.shape[-1] // 2
    return jnp.concatenate([t[..., half:], t[..., :half]], axis=-1)


def _mla_projection(x, w_a, g_q, g_kv, w_uq, w_ukv, seq, tm=512):
    t, d = x.shape
    nh, nope, rope, vd = MLA_HEADS, MLA_NOPE, MLA_ROPE, MLA_V
    qrank, kvrank = MLA_Q_RANK, MLA_KV_RANK
    tm = min(tm, seq)
    rep = LANES // rope
    w_kr = w_a[:, qrank + kvrank:]
    wa_ext = jnp.concatenate([w_a[:, :qrank + kvrank], jnp.tile(w_kr, (1, rep)),
                              jnp.tile(_swap_halves(w_kr), (1, rep))], axis=1).astype(BF16)
    wq = w_uq.reshape(qrank, nh, nope + rope)
    wq_rope = wq[:, :, nope:]
    wuq_ext = jnp.concatenate([wq[:, :, :nope].reshape(qrank, nh * nope),
                               wq_rope.reshape(qrank, nh * rope),
                               _swap_halves(wq_rope).reshape(qrank, nh * rope)], axis=1).astype(BF16)
    wkv = w_ukv.reshape(kvrank, nh, nope + vd)
    wukv_p = jnp.concatenate([wkv[:, :, :nope].reshape(kvrank, nh * nope),
                              wkv[:, :, nope:].reshape(kvrank, nh * vd)], axis=1).astype(BF16)
    inv_freq = 1.0 / (ROPE_THETA ** (jnp.arange(0, rope, 2, dtype=F32) / rope))
    ang = jnp.arange(seq, dtype=F32)[:, None] * inv_freq[None, :]
    cos, sin = jnp.cos(ang), jnp.sin(ang)
    cos_t = jnp.tile(jnp.concatenate([cos, cos], axis=-1), (1, rep))
    sin_t = jnp.tile(jnp.concatenate([-sin, sin], axis=-1), (1, rep))
    nblk = seq // tm
    scale = (nope + rope) ** -0.5 * LOG2E
    na = wa_ext.shape[1]
    nq = wuq_ext.shape[1]
    nkv = wukv_p.shape[1]
    const = lambda i: (0, 0)
    tok = lambda i: (i, 0)
    return pl.pallas_call(
        functools.partial(_mla_proj_kernel, scale=scale),
        grid=(t // tm,),
        in_specs=[pl.BlockSpec((tm, d), tok),
                  pl.BlockSpec((d, na), const),
                  pl.BlockSpec((1, qrank), const),
                  pl.BlockSpec((1, kvrank), const),
                  pl.BlockSpec((qrank, nq), const),
                  pl.BlockSpec((kvrank, nkv), const),
                  pl.BlockSpec((tm, LANES), lambda i: (i % nblk, 0)),
                  pl.BlockSpec((tm, LANES), lambda i: (i % nblk, 0))],
        out_specs=[pl.BlockSpec((tm, nh * nope), tok),
                   pl.BlockSpec((tm, nh * rope), tok),
                   pl.BlockSpec((tm, nh * nope), tok),
                   pl.BlockSpec((tm, LANES), tok),
                   pl.BlockSpec((tm, nh * vd), tok)],
        out_shape=[jax.ShapeDtypeStruct((t, nh * nope), BF16),
                   jax.ShapeDtypeStruct((t, nh * rope), BF16),
                   jax.ShapeDtypeStruct((t, nh * nope), BF16),
                   jax.ShapeDtypeStruct((t, LANES), BF16),
                   jax.ShapeDtypeStruct((t, nh * vd), BF16)],
        compiler_params=_params("parallel"),
        name="mla_proj",
    )(x, wa_ext, g_q.reshape(1, qrank), g_kv.reshape(1, kvrank), wuq_ext, wukv_p, cos_t, sin_t)


def _mla_attn_kernel(qn_ref, qr_ref, kn_ref, kr_ref, v_ref, o_ref, kcat_ref, *, n_sub):
    npair = kcat_ref.shape[0]
    nhead = 2 * npair
    rows = qn_ref.shape[1] // n_sub

    @pl.when(pl.program_id(2) == 0)
    def _():
        for pp in range(npair):
            kcat_ref[pp, :, :LANES] = kn_ref[0, :, pp * LANES:(pp + 1) * LANES]
            kcat_ref[pp, :, LANES:] = kr_ref[0]

    lane = lax.broadcasted_iota(jnp.int32, (1, LANES), 1)

    def scores(chain):
        sub, head = chain // nhead, chain % nhead
        pp, e = head // 2, head % 2
        qn = qn_ref[0, sub * rows:(sub + 1) * rows, pp * LANES:(pp + 1) * LANES]
        qr = qr_ref[0, sub * rows:(sub + 1) * rows, :]
        mn = jnp.where((lane >= e * MLA_NOPE) & (lane < (e + 1) * MLA_NOPE), 1.0, 0.0).astype(BF16)
        mr = jnp.where((lane >= head * MLA_ROPE) & (lane < (head + 1) * MLA_ROPE), 1.0, 0.0).astype(BF16)
        lhs = jnp.concatenate([qn * mn, qr * mr], axis=1)
        return _nt_dot(lhs, kcat_ref[pp])

    nchain = n_sub * nhead
    sc_next = scores(0)
    prev = None
    for chain in range(nchain):
        sc = sc_next
        if chain + 1 < nchain:
            sc_next = scores(chain + 1)
        sub, head = chain // nhead, chain % nhead
        pp = head // 2
        pe = jnp.exp2(sc - jnp.max(sc, axis=-1, keepdims=True))
        rl = 1.0 / jnp.sum(pe, axis=-1, keepdims=True)
        v = v_ref[0, :, pp * LANES:(pp + 1) * LANES]
        out = jnp.dot(pe.astype(BF16), v, preferred_element_type=F32) * rl
        if head % 2 == 1:
            o_ref[0, sub * rows:(sub + 1) * rows, pp * LANES:(pp + 1) * LANES] = jnp.where(
                lane < MLA_V, prev, out).astype(o_ref.dtype)
        prev = out


def _mla_attention(qn, qr, kn, kr, v, tq=2048, n_sub=4, heads_per_step=4):
    b, s, _ = qn.shape
    ngrp = MLA_HEADS // heads_per_step
    npair = heads_per_step // 2
    wn = npair * LANES
    assert heads_per_step * MLA_ROPE == LANES
    tq = min(tq, s)
    return pl.pallas_call(
        functools.partial(_mla_attn_kernel, n_sub=n_sub),
        grid=(b, ngrp, s // tq),
        in_specs=[pl.BlockSpec((1, tq, wn), lambda i, g, j: (i, j, g)),
                  pl.BlockSpec((1, tq, LANES), lambda i, g, j: (i, j, g)),
                  pl.BlockSpec((1, s, wn), lambda i, g, j: (i, 0, g)),
                  pl.BlockSpec((1, s, LANES), lambda i, g, j: (i, 0, 0)),
                  pl.BlockSpec((1, s, wn), lambda i, g, j: (i, 0, g))],
        out_specs=pl.BlockSpec((1, tq, wn), lambda i, g, j: (i, j, g)),
        out_shape=jax.ShapeDtypeStruct((b, s, ngrp * wn), BF16),
        scratch_shapes=[pltpu.VMEM((npair, s, 2 * LANES), BF16)],
        compiler_params=_params("parallel", "parallel", "arbitrary", vmem=VMEM_LIMIT_MLA),
        name="mla_attn",
    )(qn, qr, kn, kr, v)


def _scaled_qkv_weight(w_qkv, head_dim):
    d = w_qkv.shape[0]
    c = head_dim ** -0.5 * LOG2E
    return jnp.concatenate([w_qkv[:, :d] * c, w_qkv[:, d:]], axis=1).astype(BF16)


def kernel(x, conv_w_in, conv_w, conv_w_out, diff_w_qkv, diff_lambda, diff_subln_g, diff_w_out,
           na_w_qkv, na_rpb, na_w_out, mla_w_a, mla_g_q, mla_g_kv, mla_w_uq, mla_w_ukv, mla_w_out,
           ln1_g, ln1_b, ffn_w_gu, ffn_w_down, ln2_g, ln2_b):
    b, s, d = x.shape
    t = b * s
    xf = x.reshape(t, d)
    for i in range(DEPTH):
        m, j = i % N_MIXERS, i // N_MIXERS
        if m == 0:
            a = _conv_mixer(xf.reshape(b, s, d), conv_w_in[j].astype(BF16), conv_w[j])
            w_out = conv_w_out[j]
        elif m == 1:
            qkv = _proj(xf, _scaled_qkv_weight(diff_w_qkv[j], d // (2 * DA_HEADS)), BF16)
            a = _diff_attention(qkv.reshape(b, s, 3 * d), diff_lambda[j], diff_subln_g[j], i)
            w_out = diff_w_out[j]
        elif m == 2:
            qkv = _proj(xf, _scaled_qkv_weight(na_w_qkv[j], d // NA_HEADS), BF16)
            a = _neighborhood_attention(qkv.reshape(b, s, 3 * d), na_rpb[j])
            w_out = na_w_out[j]
        else:
            qn, qr, kn, kr, v = _mla_projection(xf, mla_w_a[j], mla_g_q[j], mla_g_kv[j],
                                                mla_w_uq[j], mla_w_ukv[j], s)
            r3 = lambda z: z.reshape(b, s, z.shape[-1])
            a = _mla_attention(r3(qn), r3(qr), r3(kn), r3(kr), r3(v))
            w_out = mla_w_out[j]
        xf = _post_mixer(a.reshape(t, -1), w_out.astype(BF16), xf, ln1_g[i], ln1_b[i],
                         ffn_w_gu[i].astype(BF16), ffn_w_down[i].astype(BF16), ln2_g[i], ln2_b[i])
    return xf.reshape(b, s, d)
```

```python
import functools
import math

import numpy as np
import jax
import jax.numpy as jnp
from jax import lax
from jax.experimental import pallas as pl
from jax.experimental.pallas import tpu as pltpu

DEPTH = 4
N_MIXERS = 4
GRID_W = 64
LN_EPS = 1e-5
RMS_EPS = 1e-6
LOG2E = math.log2(math.e)
DEEPNORM_ALPHA = (2 * DEPTH) ** 0.25
DA_HEADS = 8
NA_HEADS = 16
NA_MAX_ROWS = 8
NA_WIN_COLS = 16
MLA_HEADS = 16
MLA_Q_RANK = 256
MLA_KV_RANK = 128
MLA_NOPE = 64
MLA_ROPE = 32
MLA_V = 64
ROPE_THETA = 10000.0

LANES = 128
VMEM_LIMIT = 56 * 1024 * 1024
VMEM_LIMIT_MLA = 60 * 1024 * 1024
BF16 = jnp.bfloat16
F32 = jnp.float32


def _params(*sem, vmem=VMEM_LIMIT):
    return pltpu.CompilerParams(dimension_semantics=sem, vmem_limit_bytes=vmem)


def _layer_norm_rows(y, g, b):
    mu = jnp.mean(y, axis=-1, keepdims=True)
    yc = y - mu
    var = jnp.mean(yc * yc, axis=-1, keepdims=True)
    return yc * lax.rsqrt(var + LN_EPS) * g + b


def _nt_dot(a, b):
    return lax.dot_general(a, b, (((1,), (1,)), ((), ())), preferred_element_type=F32)


def _proj_kernel(x_ref, w_ref, o_ref):
    o_ref[...] = jnp.dot(x_ref[...].astype(BF16), w_ref[...], preferred_element_type=F32).astype(o_ref.dtype)


def _proj(x, w, out_dtype, tm=1024):
    t, k = x.shape
    n = w.shape[1]
    tm = min(tm, t)
    return pl.pallas_call(
        _proj_kernel,
        grid=(t // tm,),
        in_specs=[pl.BlockSpec((tm, k), lambda i: (i, 0)),
                  pl.BlockSpec((k, n), lambda i: (0, 0), pipeline_mode=pl.Buffered(1))],
        out_specs=pl.BlockSpec((tm, n), lambda i: (i, 0)),
        out_shape=jax.ShapeDtypeStruct((t, n), out_dtype),
        compiler_params=_params("parallel"),
        name="proj",
    )(x, w)


def _post_mixer_kernel(a_ref, wo_ref, x_ref, g1_ref, b1_ref, wgu_ref, wd_ref, g2_ref, b2_ref, o_ref,
                       *, d_ff, tf, n_sub):
    rows = x_ref.shape[0] // n_sub
    hs = [jnp.dot(a_ref[i * rows:(i + 1) * rows, :], wo_ref[...], preferred_element_type=F32)
          for i in range(n_sub)]
    for i in range(n_sub):
        sl = slice(i * rows, (i + 1) * rows)
        x1 = _layer_norm_rows(DEEPNORM_ALPHA * x_ref[sl, :] + hs[i], g1_ref[...], b1_ref[...])
        xb = x1.astype(BF16)
        acc = jnp.zeros(x1.shape, F32)
        for c in range(d_ff // tf):
            gate = jnp.dot(xb, wgu_ref[:, c * tf:(c + 1) * tf], preferred_element_type=F32)
            up = jnp.dot(xb, wgu_ref[:, d_ff + c * tf:d_ff + (c + 1) * tf], preferred_element_type=F32)
            hid = (gate * jax.nn.sigmoid(gate) * up).astype(BF16)
            acc = acc + jnp.dot(hid, wd_ref[c * tf:(c + 1) * tf, :], preferred_element_type=F32)
        o_ref[sl, :] = _layer_norm_rows(DEEPNORM_ALPHA * x1 + acc, g2_ref[...], b2_ref[...])


def _post_mixer(a, w_out, x, g1, b1, w_gu, w_down, g2, b2, tm=1024, tf=256, n_sub=2):
    t, d = x.shape
    k = a.shape[1]
    d_ff = w_down.shape[0]
    tm = min(tm, t)
    tok = lambda i: (i, 0)
    const = lambda i: (0, 0)
    vec = pl.BlockSpec((1, d), const)
    resident = dict(pipeline_mode=pl.Buffered(1))
    return pl.pallas_call(
        functools.partial(_post_mixer_kernel, d_ff=d_ff, tf=tf, n_sub=n_sub),
        grid=(t // tm,),
        in_specs=[pl.BlockSpec((tm, k), tok),
                  pl.BlockSpec((k, d), const, **resident),
                  pl.BlockSpec((tm, d), tok),
                  vec, vec,
                  pl.BlockSpec((d, 2 * d_ff), const, **resident),
                  pl.BlockSpec((d_ff, d), const, **resident),
                  vec, vec],
        out_specs=pl.BlockSpec((tm, d), tok),
        out_shape=jax.ShapeDtypeStruct((t, d), F32),
        compiler_params=_params("parallel"),
        name="post_mixer",
    )(a, w_out, x, g1.reshape(1, d), b1.reshape(1, d), w_gu, w_down, g2.reshape(1, d), b2.reshape(1, d))


def _conv_kernel(x_ref, wb_ref, wc_ref, wh_ref, cw_ref, o_ref, xb_ref, *, n_sub):
    @pl.when(pl.program_id(1) == 0)
    def _():
        xb_ref[...] = x_ref[0].astype(BF16)

    xb = xb_ref[...]
    s = xb.shape[0]
    tc = wb_ref.shape[1] // n_sub
    row = lax.broadcasted_iota(jnp.int32, (s, 1), 0)
    cw = cw_ref[...]

    def gates(i):
        cols = slice(i * tc, (i + 1) * tc)
        return [jnp.dot(xb, w[:, cols], preferred_element_type=F32) for w in (wb_ref, wc_ref, wh_ref)]

    nxt_g = gates(0)
    for i in range(n_sub):
        bg, cg, hh = nxt_g
        if i + 1 < n_sub:
            nxt_g = gates(i + 1)
        cols = slice(i * tc, (i + 1) * tc)
        u = cg * hh
        prev = jnp.where(row == 0, 0.0, pltpu.roll(u, 1, axis=0))
        nxt = jnp.where(row == s - 1, 0.0, pltpu.roll(u, s - 1, axis=0))
        y = cw[0:1, cols] * prev + cw[1:2, cols] * u + cw[2:3, cols] * nxt
        o_ref[0, :, cols] = (bg * y).astype(o_ref.dtype)


def _conv_mixer(x3, w_in, conv_w, tc=512, n_sub=2):
    b, s, d = x3.shape
    nj = d // tc
    return pl.pallas_call(
        functools.partial(_conv_kernel, n_sub=n_sub),
        grid=(b, nj),
        in_specs=[pl.BlockSpec((1, s, d), lambda i, j: (i, 0, 0)),
                  pl.BlockSpec((d, tc), lambda i, j: (0, j)),
                  pl.BlockSpec((d, tc), lambda i, j: (0, nj + j)),
                  pl.BlockSpec((d, tc), lambda i, j: (0, 2 * nj + j)),
                  pl.BlockSpec((3, tc), lambda i, j: (0, j))],
        out_specs=pl.BlockSpec((1, s, tc), lambda i, j: (i, 0, j)),
        out_shape=jax.ShapeDtypeStruct((b, s, d), BF16),
        scratch_shapes=[pltpu.VMEM((s, d), BF16)],
        compiler_params=_params("parallel", "arbitrary"),
        name="conv_mixer",
    )(x3, w_in, w_in, w_in, conv_w)


def _diff_attn_kernel(q_ref, k_ref, v_ref, lam_ref, slope_ref, dist_ref, g_ref, o_ref, bias_ref,
                      *, lam_init, n_chain):
    tq = q_ref.shape[1]
    s = k_ref.shape[1]
    hd = LANES // 2
    rows = tq // n_chain
    nkb = s // rows

    @pl.when((pl.program_id(1) == 0) & (pl.program_id(2) == 0))
    def _():
        bias_ref[...] = dist_ref[...] * (slope_ref[0][:, 0:1] * LOG2E)

    lane = lax.broadcasted_iota(jnp.int32, (1, LANES), 1)
    m1 = jnp.where(lane < hd, 1.0, 0.0).astype(BF16)
    m2 = jnp.where(lane >= hd, 1.0, 0.0).astype(BF16)
    lam = lam_ref[...]
    lam_full = (jnp.exp(jnp.sum(lam[0:1] * lam[1:2], axis=-1, keepdims=True))
                - jnp.exp(jnp.sum(lam[2:3] * lam[3:4], axis=-1, keepdims=True)) + lam_init)
    k = k_ref[0]

    def scores(c):
        q = q_ref[0, c * rows:(c + 1) * rows, :]
        return _nt_dot(q * m1, k), _nt_dot(q * m2, k)

    v = v_ref[0]
    sc_next = scores(0)
    for c in range(n_chain):
        sc = sc_next
        if c + 1 < n_chain:
            sc_next = scores(c + 1)
        first = nkb - 1 - (pl.program_id(2) * n_chain + c)
        bias = jnp.concatenate([bias_ref[first + kb] for kb in range(nkb)], axis=1)
        s1 = sc[0] - bias
        s2 = sc[1] - bias
        e1 = jnp.exp2(s1 - jnp.max(s1, axis=-1, keepdims=True))
        e2 = jnp.exp2(s2 - jnp.max(s2, axis=-1, keepdims=True))
        l1 = jnp.sum(e1, axis=-1, keepdims=True)
        l2 = jnp.sum(e2, axis=-1, keepdims=True)
        a = (e1 - e2 * (lam_full * l1 / l2)).astype(BF16)
        o = jnp.dot(a, v, preferred_element_type=F32) * (1.0 / l1)
        ms = jnp.mean(o * o, axis=-1, keepdims=True)
        o = o * lax.rsqrt(ms + RMS_EPS) * g_ref[...] * (1.0 - lam_init)
        o_ref[0, c * rows:(c + 1) * rows, :] = o.astype(o_ref.dtype)


def _alibi_distance_blocks(s, rows):
    nblk = 2 * s // rows - 1
    r = np.arange(rows)[None, :, None]
    col = (np.arange(nblk)[:, None, None] * rows + np.arange(rows)[None, None, :])
    return jnp.asarray(np.abs(r - col + s - rows).astype(np.float32))


def _diff_attention(qkv, lam, subln_g, layer_idx, tq=2048, n_chain=8):
    b, s, d3 = qkv.shape
    d = d3 // 3
    nh = d // LANES
    tq = min(tq, s)
    rows = tq // n_chain
    dist = _alibi_distance_blocks(s, rows)
    lam_init = 0.8 - 0.6 * math.exp(-0.3 * layer_idx)
    slopes = np.array([2.0 ** (-8.0 * (h + 1) / nh) for h in range(nh)], dtype=np.float32)
    slopes = jnp.asarray(np.broadcast_to(slopes[:, None, None], (nh, 1, LANES)).copy())
    return pl.pallas_call(
        functools.partial(_diff_attn_kernel, lam_init=lam_init, n_chain=n_chain),
        grid=(nh, b, s // tq),
        in_specs=[pl.BlockSpec((1, tq, LANES), lambda h, i, j: (i, j, h)),
                  pl.BlockSpec((1, s, LANES), lambda h, i, j: (i, 0, nh + h)),
                  pl.BlockSpec((1, s, LANES), lambda h, i, j: (i, 0, 2 * nh + h)),
                  pl.BlockSpec(lam.shape, lambda h, i, j: (0, 0)),
                  pl.BlockSpec((1, 1, LANES), lambda h, i, j: (h, 0, 0)),
                  pl.BlockSpec(dist.shape, lambda h, i, j: (0, 0, 0)),
                  pl.BlockSpec((1, LANES), lambda h, i, j: (0, 0))],
        out_specs=pl.BlockSpec((1, tq, LANES), lambda h, i, j: (i, j, h)),
        out_shape=jax.ShapeDtypeStruct((b, s, d), BF16),
        scratch_shapes=[pltpu.VMEM(dist.shape, F32)],
        compiler_params=_params("arbitrary", "arbitrary", "arbitrary"),
        name="diff_attn",
    )(qkv, qkv, qkv, lam, slopes, dist, subln_g.reshape(1, LANES))


NA_BAND = 4


def _na_kernel(q_ref, k_ref, v_ref, tbl_ref, o_ref, *, rows, kr):
    w = GRID_W
    hd = LANES // 2
    npair = q_ref.shape[2] // LANES
    nband = rows // NA_BAND
    kwin = NA_BAND + kr
    lane = lax.broadcasted_iota(jnp.int32, (1, LANES), 1)
    m0 = jnp.where(lane < hd, 1.0, 0.0).astype(BF16)
    m1 = jnp.where(lane >= hd, 1.0, 0.0).astype(BF16)

    def window_start(band):
        return min(max(band * NA_BAND - kr // 2, 0), rows - kwin)

    def scores(chain):
        pp, band = chain // nband, chain % nband
        cols = slice(pp * LANES, (pp + 1) * LANES)
        q = q_ref[0, band * NA_BAND * w:(band + 1) * NA_BAND * w, cols]
        qq = jnp.concatenate([q * m0, q * m1], axis=0)
        ws = window_start(band)
        return _nt_dot(qq, k_ref[0, ws * w:(ws + kwin) * w, cols])

    nq = NA_BAND * w
    nchain = npair * nband
    sc_next = scores(0)
    for chain in range(nchain):
        sc = sc_next
        if chain + 1 < nchain:
            sc_next = scores(chain + 1)
        pp, band = chain // nband, chain % nband
        cols = slice(pp * LANES, (pp + 1) * LANES)
        kind = 0 if band == 0 else (2 if band == nband - 1 else 1)
        sc = sc + tbl_ref[pp, kind]
        e = jnp.exp2(sc - jnp.max(sc, axis=-1, keepdims=True))
        rl = 1.0 / jnp.sum(e, axis=-1, keepdims=True)
        ws = window_start(band)
        o = jnp.dot(e.astype(BF16), v_ref[0, ws * w:(ws + kwin) * w, cols], preferred_element_type=F32) * rl
        out = jnp.where(lane < hd, o[:nq], o[nq:])
        o_ref[0, band * nq:(band + 1) * nq, cols] = out.astype(o_ref.dtype)


def _na_bias_table(rpb, rows, kr):
    kc = NA_WIN_COLS
    w = GRID_W
    kwin = NA_BAND + kr
    nband = rows // NA_BAND
    col = np.arange(w)
    cs = np.clip(col - kc // 2, 0, w - kc)
    col_ok = (col[None, :] >= cs[:, None]) & (col[None, :] < cs[:, None] + kc)
    ext = jnp.concatenate([jnp.repeat(rpb[..., :1], w - kc, axis=-1), rpb,
                           jnp.repeat(rpb[..., -1:], w - kc, axis=-1)], axis=-1)
    rel = jnp.stack([ext[..., w - 1 - qc:2 * w - 1 - qc] for qc in range(w)], axis=-2)
    bias = jnp.where(jnp.asarray(col_ok), rel * LOG2E, -jnp.inf)

    def build(bias_h):
        masked = jnp.full(bias_h[:, 0].shape, -jnp.inf, F32)
        kinds = []
        for band in (0, 1, nband - 1):
            ws = min(max(band * NA_BAND - kr // 2, 0), rows - kwin)
            row_blocks = []
            for a in range(NA_BAND):
                r = band * NA_BAND + a
                rs = min(max(r - kr // 2, 0), rows - kr)
                cols = [bias_h[:, ws + kk - r + (NA_MAX_ROWS - 1)] if rs <= ws + kk < rs + kr else masked
                        for kk in range(kwin)]
                row_blocks.append(jnp.concatenate(cols, axis=-1))
            kinds.append(jnp.concatenate(row_blocks, axis=-2))
        return jnp.stack(kinds, axis=1)

    return jnp.concatenate([build(bias[0::2]), build(bias[1::2])], axis=2)


def _neighborhood_attention(qkv, rpb, pairs_per_step=2):
    b, s, d3 = qkv.shape
    d = d3 // 3
    wl = pairs_per_step * LANES
    ngrp = d // wl
    rows = s // GRID_W
    kr = min(NA_MAX_ROWS, rows)
    assert rows % NA_BAND == 0 and rows >= NA_BAND + kr
    tbl = _na_bias_table(rpb.astype(F32), rows, kr)
    return pl.pallas_call(
        functools.partial(_na_kernel, rows=rows, kr=kr),
        grid=(ngrp, b),
        in_specs=[pl.BlockSpec((1, s, wl), lambda g, i: (i, 0, g)),
                  pl.BlockSpec((1, s, wl), lambda g, i: (i, 0, ngrp + g)),
                  pl.BlockSpec((1, s, wl), lambda g, i: (i, 0, 2 * ngrp + g)),
                  pl.BlockSpec((pairs_per_step,) + tbl.shape[1:], lambda g, i: (g, 0, 0, 0))],
        out_specs=pl.BlockSpec((1, s, wl), lambda g, i: (i, 0, g)),
        out_shape=jax.ShapeDtypeStruct((b, s, d), BF16),
        compiler_params=_params("parallel", "parallel"),
        name="na_attn",
    )(qkv, qkv, qkv, tbl)


def _mla_proj_kernel(x_ref, wa_ref, gq_ref, gkv_ref, wuq_ref, wukv_ref, cos_ref, sin_ref,
                     qn_ref, qr_ref, kn_ref, kr_ref, v_ref, *, scale):
    qrank, kvrank = MLA_Q_RANK, MLA_KV_RANK
    a = jnp.dot(x_ref[...].astype(BF16), wa_ref[...], preferred_element_type=F32)
    cq = a[:, :qrank]
    cq = cq * lax.rsqrt(jnp.mean(cq * cq, axis=-1, keepdims=True) + RMS_EPS) * gq_ref[...]
    ckv = a[:, qrank:qrank + kvrank]
    ckv = ckv * lax.rsqrt(jnp.mean(ckv * ckv, axis=-1, keepdims=True) + RMS_EPS) * gkv_ref[...]
    cos1 = cos_ref[...]
    sin1 = sin_ref[...]
    o = qrank + kvrank
    kr_ref[...] = (a[:, o:o + LANES] * cos1 + a[:, o + LANES:o + 2 * LANES] * sin1).astype(kr_ref.dtype)
    reps = qr_ref.shape[1] // LANES
    cos = jnp.concatenate([cos1] * reps, axis=1)
    sin = jnp.concatenate([sin1] * reps, axis=1)
    q = jnp.dot(cq.astype(BF16), wuq_ref[...], preferred_element_type=F32)
    nn = qn_ref.shape[1]
    nr = qr_ref.shape[1]
    qn_ref[...] = (q[:, :nn] * scale).astype(qn_ref.dtype)
    qr_ref[...] = ((q[:, nn:nn + nr] * cos + q[:, nn + nr:nn + 2 * nr] * sin) * scale).astype(qr_ref.dtype)
    kv = jnp.dot(ckv.astype(BF16), wukv_ref[...], preferred_element_type=F32)
    kn_ref[...] = kv[:, :nn].astype(kn_ref.dtype)
    v_ref[...] = kv[:, nn:].astype(v_ref.dtype)


def _swap_halves(t):
    half = t.shape[-1] // 2
    return jnp.concatenate([t[..., half:], t[..., :half]], axis=-1)


def _mla_projection(x, w_a, g_q, g_kv, w_uq, w_ukv, seq, tm=512):
    t, d = x.shape
    nh, nope, rope, vd = MLA_HEADS, MLA_NOPE, MLA_ROPE, MLA_V
    qrank, kvrank = MLA_Q_RANK, MLA_KV_RANK
    tm = min(tm, seq)
    rep = LANES // rope
    w_kr = w_a[:, qrank + kvrank:]
    wa_ext = jnp.concatenate([w_a[:, :qrank + kvrank], jnp.tile(w_kr, (1, rep)),
                              jnp.tile(_swap_halves(w_kr), (1, rep))], axis=1).astype(BF16)
    wq = w_uq.reshape(qrank, nh, nope + rope)
    wq_rope = wq[:, :, nope:]
    wuq_ext = jnp.concatenate([wq[:, :, :nope].reshape(qrank, nh * nope),
                               wq_rope.reshape(qrank, nh * rope),
                               _swap_halves(wq_rope).reshape(qrank, nh * rope)], axis=1).astype(BF16)
    wkv = w_ukv.reshape(kvrank, nh, nope + vd)
    wukv_p = jnp.concatenate([wkv[:, :, :nope].reshape(kvrank, nh * nope),
                              wkv[:, :, nope:].reshape(kvrank, nh * vd)], axis=1).astype(BF16)
    inv_freq = 1.0 / (ROPE_THETA ** (jnp.arange(0, rope, 2, dtype=F32) / rope))
    ang = jnp.arange(seq, dtype=F32)[:, None] * inv_freq[None, :]
    cos, sin = jnp.cos(ang), jnp.sin(ang)
    cos_t = jnp.tile(jnp.concatenate([cos, cos], axis=-1), (1, rep))
    sin_t = jnp.tile(jnp.concatenate([-sin, sin], axis=-1), (1, rep))
    nblk = seq // tm
    scale = (nope + rope) ** -0.5 * LOG2E
    na = wa_ext.shape[1]
    nq = wuq_ext.shape[1]
    nkv = wukv_p.shape[1]
    const = lambda i: (0, 0)
    tok = lambda i: (i, 0)
    return pl.pallas_call(
        functools.partial(_mla_proj_kernel, scale=scale),
        grid=(t // tm,),
        in_specs=[pl.BlockSpec((tm, d), tok),
                  pl.BlockSpec((d, na), const),
                  pl.BlockSpec((1, qrank), const),
                  pl.BlockSpec((1, kvrank), const),
                  pl.BlockSpec((qrank, nq), const),
                  pl.BlockSpec((kvrank, nkv), const),
                  pl.BlockSpec((tm, LANES), lambda i: (i % nblk, 0)),
                  pl.BlockSpec((tm, LANES), lambda i: (i % nblk, 0))],
        out_specs=[pl.BlockSpec((tm, nh * nope), tok),
                   pl.BlockSpec((tm, nh * rope), tok),
                   pl.BlockSpec((tm, nh * nope), tok),
                   pl.BlockSpec((tm, LANES), tok),
                   pl.BlockSpec((tm, nh * vd), tok)],
        out_shape=[jax.ShapeDtypeStruct((t, nh * nope), BF16),
                   jax.ShapeDtypeStruct((t, nh * rope), BF16),
                   jax.ShapeDtypeStruct((t, nh * nope), BF16),
                   jax.ShapeDtypeStruct((t, LANES), BF16),
                   jax.ShapeDtypeStruct((t, nh * vd), BF16)],
        compiler_params=_params("parallel"),
        name="mla_proj",
    )(x, wa_ext, g_q.reshape(1, qrank), g_kv.reshape(1, kvrank), wuq_ext, wukv_p, cos_t, sin_t)


def _mla_attn_kernel(qn_ref, qr_ref, kn_ref, kr_ref, v_ref, o_ref, kcat_ref, *, n_sub):
    npair = kcat_ref.shape[0]
    nhead = 2 * npair
    rows = qn_ref.shape[1] // n_sub

    @pl.when(pl.program_id(2) == 0)
    def _():
        for pp in range(npair):
            kcat_ref[pp, :, :LANES] = kn_ref[0, :, pp * LANES:(pp + 1) * LANES]
            kcat_ref[pp, :, LANES:] = kr_ref[0]

    lane = lax.broadcasted_iota(jnp.int32, (1, LANES), 1)

    def scores(chain):
        sub, head = chain // nhead, chain % nhead
        pp, e = head // 2, head % 2
        qn = qn_ref[0, sub * rows:(sub + 1) * rows, pp * LANES:(pp + 1) * LANES]
        qr = qr_ref[0, sub * rows:(sub + 1) * rows, :]
        mn = jnp.where((lane >= e * MLA_NOPE) & (lane < (e + 1) * MLA_NOPE), 1.0, 0.0).astype(BF16)
        mr = jnp.where((lane >= head * MLA_ROPE) & (lane < (head + 1) * MLA_ROPE), 1.0, 0.0).astype(BF16)
        lhs = jnp.concatenate([qn * mn, qr * mr], axis=1)
        return _nt_dot(lhs, kcat_ref[pp])

    nchain = n_sub * nhead
    sc_next = scores(0)
    prev = None
    for chain in range(nchain):
        sc = sc_next
        if chain + 1 < nchain:
            sc_next = scores(chain + 1)
        sub, head = chain // nhead, chain % nhead
        pp = head // 2
        pe = jnp.exp2(sc - jnp.max(sc, axis=-1, keepdims=True))
        rl = 1.0 / jnp.sum(pe, axis=-1, keepdims=True)
        v = v_ref[0, :, pp * LANES:(pp + 1) * LANES]
        out = jnp.dot(pe.astype(BF16), v, preferred_element_type=F32) * rl
        if head % 2 == 1:
            o_ref[0, sub * rows:(sub + 1) * rows, pp * LANES:(pp + 1) * LANES] = jnp.where(
                lane < MLA_V, prev, out).astype(o_ref.dtype)
        prev = out


def _mla_attention(qn, qr, kn, kr, v, tq=2048, n_sub=4, heads_per_step=4):
    b, s, _ = qn.shape
    ngrp = MLA_HEADS // heads_per_step
    npair = heads_per_step // 2
    wn = npair * LANES
    assert heads_per_step * MLA_ROPE == LANES
    tq = min(tq, s)
    return pl.pallas_call(
        functools.partial(_mla_attn_kernel, n_sub=n_sub),
        grid=(b, ngrp, s // tq),
        in_specs=[pl.BlockSpec((1, tq, wn), lambda i, g, j: (i, j, g)),
                  pl.BlockSpec((1, tq, LANES), lambda i, g, j: (i, j, g)),
                  pl.BlockSpec((1, s, wn), lambda i, g, j: (i, 0, g)),
                  pl.BlockSpec((1, s, LANES), lambda i, g, j: (i, 0, 0)),
                  pl.BlockSpec((1, s, wn), lambda i, g, j: (i, 0, g))],
        out_specs=pl.BlockSpec((1, tq, wn), lambda i, g, j: (i, j, g)),
        out_shape=jax.ShapeDtypeStruct((b, s, ngrp * wn), BF16),
        scratch_shapes=[pltpu.VMEM((npair, s, 2 * LANES), BF16)],
        compiler_params=_params("parallel", "parallel", "arbitrary", vmem=VMEM_LIMIT_MLA),
        name="mla_attn",
    )(qn, qr, kn, kr, v)


def _scaled_qkv_weight(w_qkv, head_dim):
    d = w_qkv.shape[0]
    c = head_dim ** -0.5 * LOG2E
    return jnp.concatenate([w_qkv[:, :d] * c, w_qkv[:, d:]], axis=1).astype(BF16)


def kernel(x, conv_w_in, conv_w, conv_w_out, diff_w_qkv, diff_lambda, diff_subln_g, diff_w_out,
           na_w_qkv, na_rpb, na_w_out, mla_w_a, mla_g_q, mla_g_kv, mla_w_uq, mla_w_ukv, mla_w_out,
           ln1_g, ln1_b, ffn_w_gu, ffn_w_down, ln2_g, ln2_b):
    b, s, d = x.shape
    t = b * s
    xf = x.reshape(t, d)
    for i in range(DEPTH):
        m, j = i % N_MIXERS, i // N_MIXERS
        if m == 0:
            a = _conv_mixer(xf.reshape(b, s, d), conv_w_in[j].astype(BF16), conv_w[j])
            w_out = conv_w_out[j]
        elif m == 1:
            qkv = _proj(xf, _scaled_qkv_weight(diff_w_qkv[j], d // (2 * DA_HEADS)), BF16)
            a = _diff_attention(qkv.reshape(b, s, 3 * d), diff_lambda[j], diff_subln_g[j], i)
            w_out = diff_w_out[j]
        elif m == 2:
            qkv = _proj(xf, _scaled_qkv_weight(na_w_qkv[j], d // NA_HEADS), BF16)
            a = _neighborhood_attention(qkv.reshape(b, s, 3 * d), na_rpb[j])
            w_out = na_w_out[j]
        else:
            qn, qr, kn, kr, v = _mla_projection(xf, mla_w_a[j], mla_g_q[j], mla_g_kv[j],
                                                mla_w_uq[j], mla_w_ukv[j], s)
            r3 = lambda z: z.reshape(b, s, z.shape[-1])
            a = _mla_attention(r3(qn), r3(qr), r3(kn), r3(kr), r3(v))
            w_out = mla_w_out[j]
        xf = _post_mixer(a.reshape(t, -1), w_out.astype(BF16), xf, ln1_g[i], ln1_b[i],
                         ffn_w_gu[i].astype(BF16), ffn_w_down[i].astype(BF16), ln2_g[i], ln2_b[i])
    return xf.reshape(b, s, d)
```

```python
import functools
import math

import numpy as np
import jax
import jax.numpy as jnp
from jax import lax
from jax.experimental import pallas as pl
from jax.experimental.pallas import tpu as pltpu

DEPTH = 4
N_MIXERS = 4
GRID_W = 64
LN_EPS = 1e-5
RMS_EPS = 1e-6
LOG2E = math.log2(math.e)
DEEPNORM_ALPHA = (2 * DEPTH) ** 0.25
DA_HEADS = 8
NA_HEADS = 16
NA_MAX_ROWS = 8
NA_WIN_COLS = 16
MLA_HEADS = 16
MLA_Q_RANK = 256
MLA_KV_RANK = 128
MLA_NOPE = 64
MLA_ROPE = 32
MLA_V = 64
ROPE_THETA = 10000.0

LANES = 128
VMEM_LIMIT = 56 * 1024 * 1024
VMEM_LIMIT_MLA = 60 * 1024 * 1024
BF16 = jnp.bfloat16
F32 = jnp.float32


def _params(*sem, vmem=VMEM_LIMIT):
    return pltpu.CompilerParams(dimension_semantics=sem, vmem_limit_bytes=vmem)


def _layer_norm_rows(y, g, b):
    mu = jnp.mean(y, axis=-1, keepdims=True)
    yc = y - mu
    var = jnp.mean(yc * yc, axis=-1, keepdims=True)
    return yc * lax.rsqrt(var + LN_EPS) * g + b


def _nt_dot(a, b):
    return lax.dot_general(a, b, (((1,), (1,)), ((), ())), preferred_element_type=F32)


def _proj_kernel(x_ref, w_ref, o_ref):
    o_ref[...] = jnp.dot(x_ref[...].astype(BF16), w_ref[...], preferred_element_type=F32).astype(o_ref.dtype)


def _proj(x, w, out_dtype, tm=1024):
    t, k = x.shape
    n = w.shape[1]
    tm = min(tm, t)
    return pl.pallas_call(
        _proj_kernel,
        grid=(t // tm,),
        in_specs=[pl.BlockSpec((tm, k), lambda i: (i, 0)),
                  pl.BlockSpec((k, n), lambda i: (0, 0), pipeline_mode=pl.Buffered(1))],
        out_specs=pl.BlockSpec((tm, n), lambda i: (i, 0)),
        out_shape=jax.ShapeDtypeStruct((t, n), out_dtype),
        compiler_params=_params("parallel"),
        name="proj",
    )(x, w)


def _post_mixer_kernel(a_ref, wo_ref, x_ref, g1_ref, b1_ref, wgu_ref, wd_ref, g2_ref, b2_ref, o_ref,
                       *, d_ff, tf, n_sub):
    rows = x_ref.shape[0] // n_sub
    hs = [jnp.dot(a_ref[i * rows:(i + 1) * rows, :], wo_ref[...], preferred_element_type=F32)
          for i in range(n_sub)]
    for i in range(n_sub):
        sl = slice(i * rows, (i + 1) * rows)
        x1 = _layer_norm_rows(DEEPNORM_ALPHA * x_ref[sl, :] + hs[i], g1_ref[...], b1_ref[...])
        xb = x1.astype(BF16)
        acc = jnp.zeros(x1.shape, F32)
        for c in range(d_ff // tf):
            gate = jnp.dot(xb, wgu_ref[:, c * tf:(c + 1) * tf], preferred_element_type=F32)
            up = jnp.dot(xb, wgu_ref[:, d_ff + c * tf:d_ff + (c + 1) * tf], preferred_element_type=F32)
            hid = (gate * jax.nn.sigmoid(gate) * up).astype(BF16)
            acc = acc + jnp.dot(hid, wd_ref[c * tf:(c + 1) * tf, :], preferred_element_type=F32)
        o_ref[sl, :] = _layer_norm_rows(DEEPNORM_ALPHA * x1 + acc, g2_ref[...], b2_ref[...])


def _post_mixer(a, w_out, x, g1, b1, w_gu, w_down, g2, b2, tm=1024, tf=256, n_sub=2):
    t, d = x.shape
    k = a.shape[1]
    d_ff = w_down.shape[0]
    tm = min(tm, t)
    tok = lambda i: (i, 0)
    const = lambda i: (0, 0)
    vec = pl.BlockSpec((1, d), const)
    resident = dict(pipeline_mode=pl.Buffered(1))
    return pl.pallas_call(
        functools.partial(_post_mixer_kernel, d_ff=d_ff, tf=tf, n_sub=n_sub),
        grid=(t // tm,),
        in_specs=[pl.BlockSpec((tm, k), tok),
                  pl.BlockSpec((k, d), const, **resident),
                  pl.BlockSpec((tm, d), tok),
                  vec, vec,
                  pl.BlockSpec((d, 2 * d_ff), const, **resident),
                  pl.BlockSpec((d_ff, d), const, **resident),
                  vec, vec],
        out_specs=pl.BlockSpec((tm, d), tok),
        out_shape=jax.ShapeDtypeStruct((t, d), F32),
        compiler_params=_params("parallel"),
        name="post_mixer",
    )(a, w_out, x, g1.reshape(1, d), b1.reshape(1, d), w_gu, w_down, g2.reshape(1, d), b2.reshape(1, d))


def _conv_kernel(x_ref, wb_ref, wc_ref, wh_ref, cw_ref, o_ref, xb_ref, *, n_sub):
    @pl.when(pl.program_id(1) == 0)
    def _():
        xb_ref[...] = x_ref[0].astype(BF16)

    xb = xb_ref[...]
    s = xb.shape[0]
    tc = wb_ref.shape[1] // n_sub
    row = lax.broadcasted_iota(jnp.int32, (s, 1), 0)
    cw = cw_ref[...]

    def gates(i):
        cols = slice(i * tc, (i + 1) * tc)
        return [jnp.dot(xb, w[:, cols], preferred_element_type=F32) for w in (wb_ref, wc_ref, wh_ref)]

    nxt_g = gates(0)
    for i in range(n_sub):
        bg, cg, hh = nxt_g
        if i + 1 < n_sub:
            nxt_g = gates(i + 1)
        cols = slice(i * tc, (i + 1) * tc)
        u = cg * hh
        prev = jnp.where(row == 0, 0.0, pltpu.roll(u, 1, axis=0))
        nxt = jnp.where(row == s - 1, 0.0, pltpu.roll(u, s - 1, axis=0))
        y = cw[0:1, cols] * prev + cw[1:2, cols] * u + cw[2:3, cols] * nxt
        o_ref[0, :, cols] = (bg * y).astype(o_ref.dtype)


def _conv_mixer(x3, w_in, conv_w, tc=512, n_sub=2):
    b, s, d = x3.shape
    nj = d // tc
    return pl.pallas_call(
        functools.partial(_conv_kernel, n_sub=n_sub),
        grid=(b, nj),
        in_specs=[pl.BlockSpec((1, s, d), lambda i, j: (i, 0, 0)),
                  pl.BlockSpec((d, tc), lambda i, j: (0, j)),
                  pl.BlockSpec((d, tc), lambda i, j: (0, nj + j)),
                  pl.BlockSpec((d, tc), lambda i, j: (0, 2 * nj + j)),
                  pl.BlockSpec((3, tc), lambda i, j: (0, j))],
        out_specs=pl.BlockSpec((1, s, tc), lambda i, j: (i, 0, j)),
        out_shape=jax.ShapeDtypeStruct((b, s, d), BF16),
        scratch_shapes=[pltpu.VMEM((s, d), BF16)],
        compiler_params=_params("parallel", "arbitrary"),
        name="conv_mixer",
    )(x3, w_in, w_in, w_in, conv_w)


def _diff_attn_kernel(q_ref, k_ref, v_ref, lam_ref, slope_ref, dist_ref, g_ref, o_ref, bias_ref,
                      *, lam_init, n_chain):
    tq = q_ref.shape[1]
    s = k_ref.shape[1]
    hd = LANES // 2
    rows = tq // n_chain
    nkb = s // rows

    @pl.when((pl.program_id(1) == 0) & (pl.program_id(2) == 0))
    def _():
        bias_ref[...] = dist_ref[...] * (slope_ref[0][:, 0:1] * LOG2E)

    lane = lax.broadcasted_iota(jnp.int32, (1, LANES), 1)
    m1 = jnp.where(lane < hd, 1.0, 0.0).astype(BF16)
    m2 = jnp.where(lane >= hd, 1.0, 0.0).astype(BF16)
    lam = lam_ref[...]
    lam_full = (jnp.exp(jnp.sum(lam[0:1] * lam[1:2], axis=-1, keepdims=True))
                - jnp.exp(jnp.sum(lam[2:3] * lam[3:4], axis=-1, keepdims=True)) + lam_init)
    k = k_ref[0]

    def scores(c):
        q = q_ref[0, c * rows:(c + 1) * rows, :]
        return _nt_dot(q * m1, k), _nt_dot(q * m2, k)

    v = v_ref[0]
    sc_next = scores(0)
    for c in range(n_chain):
        sc = sc_next
        if c + 1 < n_chain:
            sc_next = scores(c + 1)
        first = nkb - 1 - (pl.program_id(2) * n_chain + c)
        bias = jnp.concatenate([bias_ref[first + kb] for kb in range(nkb)], axis=1)
        s1 = sc[0] - bias
        s2 = sc[1] - bias
        e1 = jnp.exp2(s1 - jnp.max(s1, axis=-1, keepdims=True))
        e2 = jnp.exp2(s2 - jnp.max(s2, axis=-1, keepdims=True))
        l1 = jnp.sum(e1, axis=-1, keepdims=True)
        l2 = jnp.sum(e2, axis=-1, keepdims=True)
        a = (e1 - e2 * (lam_full * l1 / l2)).astype(BF16)
        o = jnp.dot(a, v, preferred_element_type=F32) * (1.0 / l1)
        ms = jnp.mean(o * o, axis=-1, keepdims=True)
        o = o * lax.rsqrt(ms + RMS_EPS) * g_ref[...] * (1.0 - lam_init)
        o_ref[0, c * rows:(c + 1) * rows, :] = o.astype(o_ref.dtype)


def _alibi_distance_blocks(s, rows):
    nblk = 2 * s // rows - 1
    r = np.arange(rows)[None, :, None]
    col = (np.arange(nblk)[:, None, None] * rows + np.arange(rows)[None, None, :])
    return jnp.asarray(np.abs(r - col + s - rows).astype(np.float32))


def _diff_attention(qkv, lam, subln_g, layer_idx, tq=2048, n_chain=4):
    b, s, d3 = qkv.shape
    d = d3 // 3
    nh = d // LANES
    tq = min(tq, s)
    rows = tq // n_chain
    dist = _alibi_distance_blocks(s, rows)
    lam_init = 0.8 - 0.6 * math.exp(-0.3 * layer_idx)
    slopes = np.array([2.0 ** (-8.0 * (h + 1) / nh) for h in range(nh)], dtype=np.float32)
    slopes = jnp.asarray(np.broadcast_to(slopes[:, None, None], (nh, 1, LANES)).copy())
    return pl.pallas_call(
        functools.partial(_diff_attn_kernel, lam_init=lam_init, n_chain=n_chain),
        grid=(nh, b, s // tq),
        in_specs=[pl.BlockSpec((1, tq, LANES), lambda h, i, j: (i, j, h)),
                  pl.BlockSpec((1, s, LANES), lambda h, i, j: (i, 0, nh + h)),
                  pl.BlockSpec((1, s, LANES), lambda h, i, j: (i, 0, 2 * nh + h)),
                  pl.BlockSpec(lam.shape, lambda h, i, j: (0, 0)),
                  pl.BlockSpec((1, 1, LANES), lambda h, i, j: (h, 0, 0)),
                  pl.BlockSpec(dist.shape, lambda h, i, j: (0, 0, 0)),
                  pl.BlockSpec((1, LANES), lambda h, i, j: (0, 0))],
        out_specs=pl.BlockSpec((1, tq, LANES), lambda h, i, j: (i, j, h)),
        out_shape=jax.ShapeDtypeStruct((b, s, d), BF16),
        scratch_shapes=[pltpu.VMEM(dist.shape, F32)],
        compiler_params=_params("arbitrary", "arbitrary", "arbitrary"),
        name="diff_attn",
    )(qkv, qkv, qkv, lam, slopes, dist, subln_g.reshape(1, LANES))


NA_BAND = 4


def _na_kernel(q_ref, k_ref, v_ref, tbl_ref, o_ref, *, rows, kr):
    w = GRID_W
    hd = LANES // 2
    npair = q_ref.shape[2] // LANES
    nband = rows // NA_BAND
    kwin = NA_BAND + kr
    lane = lax.broadcasted_iota(jnp.int32, (1, LANES), 1)
    m0 = jnp.where(lane < hd, 1.0, 0.0).astype(BF16)
    m1 = jnp.where(lane >= hd, 1.0, 0.0).astype(BF16)

    def window_start(band):
        return min(max(band * NA_BAND - kr // 2, 0), rows - kwin)

    def scores(chain):
        pp, band = chain // nband, chain % nband
        cols = slice(pp * LANES, (pp + 1) * LANES)
        q = q_ref[0, band * NA_BAND * w:(band + 1) * NA_BAND * w, cols]
        qq = jnp.concatenate([q * m0, q * m1], axis=0)
        ws = window_start(band)
        return _nt_dot(qq, k_ref[0, ws * w:(ws + kwin) * w, cols])

    nq = NA_BAND * w
    nchain = npair * nband
    sc_next = scores(0)
    for chain in range(nchain):
        sc = sc_next
        if chain + 1 < nchain:
            sc_next = scores(chain + 1)
        pp, band = chain // nband, chain % nband
        cols = slice(pp * LANES, (pp + 1) * LANES)
        kind = 0 if band == 0 else (2 if band == nband - 1 else 1)
        sc = sc + tbl_ref[pp, kind]
        e = jnp.exp2(sc - jnp.max(sc, axis=-1, keepdims=True))
        rl = 1.0 / jnp.sum(e, axis=-1, keepdims=True)
        ws = window_start(band)
        o = jnp.dot(e.astype(BF16), v_ref[0, ws * w:(ws + kwin) * w, cols], preferred_element_type=F32) * rl
        out = jnp.where(lane < hd, o[:nq], o[nq:])
        o_ref[0, band * nq:(band + 1) * nq, cols] = out.astype(o_ref.dtype)


def _na_bias_table(rpb, rows, kr):
    kc = NA_WIN_COLS
    w = GRID_W
    kwin = NA_BAND + kr
    nband = rows // NA_BAND
    col = np.arange(w)
    cs = np.clip(col - kc // 2, 0, w - kc)
    col_ok = (col[None, :] >= cs[:, None]) & (col[None, :] < cs[:, None] + kc)
    ext = jnp.concatenate([jnp.repeat(rpb[..., :1], w - kc, axis=-1), rpb,
                           jnp.repeat(rpb[..., -1:], w - kc, axis=-1)], axis=-1)
    rel = jnp.stack([ext[..., w - 1 - qc:2 * w - 1 - qc] for qc in range(w)], axis=-2)
    bias = jnp.where(jnp.asarray(col_ok), rel * LOG2E, -jnp.inf)

    def build(bias_h):
        masked = jnp.full(bias_h[:, 0].shape, -jnp.inf, F32)
        kinds = []
        for band in (0, 1, nband - 1):
            ws = min(max(band * NA_BAND - kr // 2, 0), rows - kwin)
            row_blocks = []
            for a in range(NA_BAND):
                r = band * NA_BAND + a
                rs = min(max(r - kr // 2, 0), rows - kr)
                cols = [bias_h[:, ws + kk - r + (NA_MAX_ROWS - 1)] if rs <= ws + kk < rs + kr else masked
                        for kk in range(kwin)]
                row_blocks.append(jnp.concatenate(cols, axis=-1))
            kinds.append(jnp.concatenate(row_blocks, axis=-2))
        return jnp.stack(kinds, axis=1)

    return jnp.concatenate([build(bias[0::2]), build(bias[1::2])], axis=2)


def _neighborhood_attention(qkv, rpb, pairs_per_step=2):
    b, s, d3 = qkv.shape
    d = d3 // 3
    wl = pairs_per_step * LANES
    ngrp = d // wl
    rows = s // GRID_W
    kr = min(NA_MAX_ROWS, rows)
    assert rows % NA_BAND == 0 and rows >= NA_BAND + kr
    tbl = _na_bias_table(rpb.astype(F32), rows, kr)
    return pl.pallas_call(
        functools.partial(_na_kernel, rows=rows, kr=kr),
        grid=(ngrp, b),
        in_specs=[pl.BlockSpec((1, s, wl), lambda g, i: (i, 0, g)),
                  pl.BlockSpec((1, s, wl), lambda g, i: (i, 0, ngrp + g)),
                  pl.BlockSpec((1, s, wl), lambda g, i: (i, 0, 2 * ngrp + g)),
                  pl.BlockSpec((pairs_per_step,) + tbl.shape[1:], lambda g, i: (g, 0, 0, 0))],
        out_specs=pl.BlockSpec((1, s, wl), lambda g, i: (i, 0, g)),
        out_shape=jax.ShapeDtypeStruct((b, s, d), BF16),
        compiler_params=_params("parallel", "parallel"),
        name="na_attn",
    )(qkv, qkv, qkv, tbl)


def _mla_proj_kernel(x_ref, wa_ref, gq_ref, gkv_ref, wuq_ref, wukv_ref, cos_ref, sin_ref,
                     qn_ref, qr_ref, kn_ref, kr_ref, v_ref, *, scale):
    qrank, kvrank = MLA_Q_RANK, MLA_KV_RANK
    a = jnp.dot(x_ref[...].astype(BF16), wa_ref[...], preferred_element_type=F32)
    cq = a[:, :qrank]
    cq = cq * lax.rsqrt(jnp.mean(cq * cq, axis=-1, keepdims=True) + RMS_EPS) * gq_ref[...]
    ckv = a[:, qrank:qrank + kvrank]
    ckv = ckv * lax.rsqrt(jnp.mean(ckv * ckv, axis=-1, keepdims=True) + RMS_EPS) * gkv_ref[...]
    cos1 = cos_ref[...]
    sin1 = sin_ref[...]
    o = qrank + kvrank
    kr_ref[...] = (a[:, o:o + LANES] * cos1 + a[:, o + LANES:o + 2 * LANES] * sin1).astype(kr_ref.dtype)
    reps = qr_ref.shape[1] // LANES
    cos = jnp.concatenate([cos1] * reps, axis=1)
    sin = jnp.concatenate([sin1] * reps, axis=1)
    q = jnp.dot(cq.astype(BF16), wuq_ref[...], preferred_element_type=F32)
    nn = qn_ref.shape[1]
    nr = qr_ref.shape[1]
    qn_ref[...] = (q[:, :nn] * scale).astype(qn_ref.dtype)
    qr_ref[...] = ((q[:, nn:nn + nr] * cos + q[:, nn + nr:nn + 2 * nr] * sin) * scale).astype(qr_ref.dtype)
    kv = jnp.dot(ckv.astype(BF16), wukv_ref[...], preferred_element_type=F32)
    kn_ref[...] = kv[:, :nn].astype(kn_ref.dtype)
    v_ref[...] = kv[:, nn:].astype(v_ref.dtype)


def _swap_halves(t):
    half = t.shape[-1] // 2
    return jnp.concatenate([t[..., half:], t[..., :half]], axis=-1)


def _mla_projection(x, w_a, g_q, g_kv, w_uq, w_ukv, seq, tm=512):
    t, d = x.shape
    nh, nope, rope, vd = MLA_HEADS, MLA_NOPE, MLA_ROPE, MLA_V
    qrank, kvrank = MLA_Q_RANK, MLA_KV_RANK
    tm = min(tm, seq)
    rep = LANES // rope
    w_kr = w_a[:, qrank + kvrank:]
    wa_ext = jnp.concatenate([w_a[:, :qrank + kvrank], jnp.tile(w_kr, (1, rep)),
                              jnp.tile(_swap_halves(w_kr), (1, rep))], axis=1).astype(BF16)
    wq = w_uq.reshape(qrank, nh, nope + rope)
    wq_rope = wq[:, :, nope:]
    wuq_ext = jnp.concatenate([wq[:, :, :nope].reshape(qrank, nh * nope),
                               wq_rope.reshape(qrank, nh * rope),
                               _swap_halves(wq_rope).reshape(qrank, nh * rope)], axis=1).astype(BF16)
    wkv = w_ukv.reshape(kvrank, nh, nope + vd)
    wukv_p = jnp.concatenate([wkv[:, :, :nope].reshape(kvrank, nh * nope),
                              wkv[:, :, nope:].reshape(kvrank, nh * vd)], axis=1).astype(BF16)
    inv_freq = 1.0 / (ROPE_THETA ** (jnp.arange(0, rope, 2, dtype=F32) / rope))
    ang = jnp.arange(seq, dtype=F32)[:, None] * inv_freq[None, :]
    cos, sin = jnp.cos(ang), jnp.sin(ang)
    cos_t = jnp.tile(jnp.concatenate([cos, cos], axis=-1), (1, rep))
    sin_t = jnp.tile(jnp.concatenate([-sin, sin], axis=-1), (1, rep))
    nblk = seq // tm
    scale = (nope + rope) ** -0.5 * LOG2E
    na = wa_ext.shape[1]
    nq = wuq_ext.shape[1]
    nkv = wukv_p.shape[1]
    const = lambda i: (0, 0)
    tok = lambda i: (i, 0)
    return pl.pallas_call(
        functools.partial(_mla_proj_kernel, scale=scale),
        grid=(t // tm,),
        in_specs=[pl.BlockSpec((tm, d), tok),
                  pl.BlockSpec((d, na), const),
                  pl.BlockSpec((1, qrank), const),
                  pl.BlockSpec((1, kvrank), const),
                  pl.BlockSpec((qrank, nq), const),
                  pl.BlockSpec((kvrank, nkv), const),
                  pl.BlockSpec((tm, LANES), lambda i: (i % nblk, 0)),
                  pl.BlockSpec((tm, LANES), lambda i: (i % nblk, 0))],
        out_specs=[pl.BlockSpec((tm, nh * nope), tok),
                   pl.BlockSpec((tm, nh * rope), tok),
                   pl.BlockSpec((tm, nh * nope), tok),
                   pl.BlockSpec((tm, LANES), tok),
                   pl.BlockSpec((tm, nh * vd), tok)],
        out_shape=[jax.ShapeDtypeStruct((t, nh * nope), BF16),
                   jax.ShapeDtypeStruct((t, nh * rope), BF16),
                   jax.ShapeDtypeStruct((t, nh * nope), BF16),
                   jax.ShapeDtypeStruct((t, LANES), BF16),
                   jax.ShapeDtypeStruct((t, nh * vd), BF16)],
        compiler_params=_params("parallel"),
        name="mla_proj",
    )(x, wa_ext, g_q.reshape(1, qrank), g_kv.reshape(1, kvrank), wuq_ext, wukv_p, cos_t, sin_t)


def _mla_attn_kernel(qn_ref, qr_ref, kn_ref, kr_ref, v_ref, o_ref, kcat_ref, *, n_sub):
    npair = kcat_ref.shape[0]
    nhead = 2 * npair
    rows = qn_ref.shape[1] // n_sub

    @pl.when(pl.program_id(2) == 0)
    def _():
        for pp in range(npair):
            kcat_ref[pp, :, :LANES] = kn_ref[0, :, pp * LANES:(pp + 1) * LANES]
            kcat_ref[pp, :, LANES:] = kr_ref[0]

    lane = lax.broadcasted_iota(jnp.int32, (1, LANES), 1)

    def scores(chain):
        sub, head = chain // nhead, chain % nhead
        pp, e = head // 2, head % 2
        qn = qn_ref[0, sub * rows:(sub + 1) * rows, pp * LANES:(pp + 1) * LANES]
        qr = qr_ref[0, sub * rows:(sub + 1) * rows, :]
        mn = jnp.where((lane >= e * MLA_NOPE) & (lane < (e + 1) * MLA_NOPE), 1.0, 0.0).astype(BF16)
        mr = jnp.where((lane >= head * MLA_ROPE) & (lane < (head + 1) * MLA_ROPE), 1.0, 0.0).astype(BF16)
        lhs = jnp.concatenate([qn * mn, qr * mr], axis=1)
        return _nt_dot(lhs, kcat_ref[pp])

    nchain = n_sub * nhead
    sc_next = scores(0)
    prev = None
    for chain in range(nchain):
        sc = sc_next
        if chain + 1 < nchain:
            sc_next = scores(chain + 1)
        sub, head = chain // nhead, chain % nhead
        pp = head // 2
        pe = jnp.exp2(sc - jnp.max(sc, axis=-1, keepdims=True))
        rl = 1.0 / jnp.sum(pe, axis=-1, keepdims=True)
        v = v_ref[0, :, pp * LANES:(pp + 1) * LANES]
        out = jnp.dot(pe.astype(BF16), v, preferred_element_type=F32) * rl
        if head % 2 == 1:
            o_ref[0, sub * rows:(sub + 1) * rows, pp * LANES:(pp + 1) * LANES] = jnp.where(
                lane < MLA_V, prev, out).astype(o_ref.dtype)
        prev = out


def _mla_attention(qn, qr, kn, kr, v, tq=2048, n_sub=4, heads_per_step=4):
    b, s, _ = qn.shape
    ngrp = MLA_HEADS // heads_per_step
    npair = heads_per_step // 2
    wn = npair * LANES
    assert heads_per_step * MLA_ROPE == LANES
    tq = min(tq, s)
    return pl.pallas_call(
        functools.partial(_mla_attn_kernel, n_sub=n_sub),
        grid=(b, ngrp, s // tq),
        in_specs=[pl.BlockSpec((1, tq, wn), lambda i, g, j: (i, j, g)),
                  pl.BlockSpec((1, tq, LANES), lambda i, g, j: (i, j, g)),
                  pl.BlockSpec((1, s, wn), lambda i, g, j: (i, 0, g)),
                  pl.BlockSpec((1, s, LANES), lambda i, g, j: (i, 0, 0)),
                  pl.BlockSpec((1, s, wn), lambda i, g, j: (i, 0, g))],
        out_specs=pl.BlockSpec((1, tq, wn), lambda i, g, j: (i, j, g)),
        out_shape=jax.ShapeDtypeStruct((b, s, ngrp * wn), BF16),
        scratch_shapes=[pltpu.VMEM((npair, s, 2 * LANES), BF16)],
        compiler_params=_params("parallel", "parallel", "arbitrary", vmem=VMEM_LIMIT_MLA),
        name="mla_attn",
    )(qn, qr, kn, kr, v)


def _scaled_qkv_weight(w_qkv, head_dim):
    d = w_qkv.shape[0]
    c = head_dim ** -0.5 * LOG2E
    return jnp.concatenate([w_qkv[:, :d] * c, w_qkv[:, d:]], axis=1).astype(BF16)


def kernel(x, conv_w_in, conv_w, conv_w_out, diff_w_qkv, diff_lambda, diff_subln_g, diff_w_out,
           na_w_qkv, na_rpb, na_w_out, mla_w_a, mla_g_q, mla_g_kv, mla_w_uq, mla_w_ukv, mla_w_out,
           ln1_g, ln1_b, ffn_w_gu, ffn_w_down, ln2_g, ln2_b):
    b, s, d = x.shape
    t = b * s
    xf = x.reshape(t, d)
    for i in range(DEPTH):
        m, j = i % N_MIXERS, i // N_MIXERS
        if m == 0:
            a = _conv_mixer(xf.reshape(b, s, d), conv_w_in[j].astype(BF16), conv_w[j])
            w_out = conv_w_out[j]
        elif m == 1:
            qkv = _proj(xf, _scaled_qkv_weight(diff_w_qkv[j], d // (2 * DA_HEADS)), BF16)
            a = _diff_attention(qkv.reshape(b, s, 3 * d), diff_lambda[j], diff_subln_g[j], i)
            w_out = diff_w_out[j]
        elif m == 2:
            qkv = _proj(xf, _scaled_qkv_weight(na_w_qkv[j], d // NA_HEADS), BF16)
            a = _neighborhood_attention(qkv.reshape(b, s, 3 * d), na_rpb[j])
            w_out = na_w_out[j]
        else:
            qn, qr, kn, kr, v = _mla_projection(xf, mla_w_a[j], mla_g_q[j], mla_g_kv[j],
                                                mla_w_uq[j], mla_w_ukv[j], s)
            r3 = lambda z: z.reshape(b, s, z.shape[-1])
            a = _mla_attention(r3(qn), r3(qr), r3(kn), r3(kr), r3(v))
            w_out = mla_w_out[j]
        xf = _post_mixer(a.reshape(t, -1), w_out.astype(BF16), xf, ln1_g[i], ln1_b[i],
                         ffn_w_gu[i].astype(BF16), ffn_w_down[i].astype(BF16), ln2_g[i], ln2_b[i])
    return xf.reshape(b, s, d)
```
